```python
import jax, jax.numpy as jnp
from jax import lax
import numpy as np

D_MODEL = 1024
BATCH = 16
SEQ = 4096
DEPTH = 4

N_META = 16
N_MIXERS = 2
N_CONV_LAYERS = (DEPTH + 1) // 2
N_GLA_LAYERS = DEPTH // 2
DN_ALPHA = (2 * DEPTH) ** 0.25
DN_BETA = (8 * DEPTH) ** -0.25
LN_EPS = 1e-5
RMS_EPS = 1e-6
CONV_CH = D_MODEL
CONV_TAPS = 31
CONV_IN = 3 * CONV_CH
GLA_HEADS = 4
GLA_DK = D_MODEL // 2
GLA_DV = D_MODEL
GLA_HK = GLA_DK // GLA_HEADS
GLA_HV = GLA_DV // GLA_HEADS
GLA_GATE_RANK = 16
GLA_TAU = 16.0
GLA_CHUNK = 64
GLA_IN = 2 * GLA_DK + 2 * GLA_DV + GLA_GATE_RANK

kernel_name = "hybrid_conformer_gla_deepnorm_meta"


def layer_norm(x, g, b):
    xf = x.astype(jnp.float32)
    mu = jnp.mean(xf, axis=-1, keepdims=True)
    xc = xf - mu
    var = jnp.mean(xc * xc, axis=-1, keepdims=True)
    y = xc * lax.rsqrt(var + LN_EPS) * g.astype(jnp.float32) + b.astype(jnp.float32)
    return y.astype(x.dtype)


def conformer_mixer(h, w_in, b_in, w_dw, b_dw, norm_g, norm_b, w_out, b_out):
    u = h @ w_in + b_in
    a, ga, z = jnp.split(u, 3, axis=-1)
    glu = a * jax.nn.sigmoid(ga)
    c = lax.conv_general_dilated(
        glu, w_dw[:, None, :].astype(glu.dtype), window_strides=(1,),
        padding=[(CONV_TAPS - 1, 0)], dimension_numbers=('NWC', 'WIO', 'NWC'),
        feature_group_count=CONV_CH) + b_dw
    c = jax.nn.silu(layer_norm(c, norm_g, norm_b))
    return (c * jax.nn.silu(z)) @ w_out + b_out


def _gla_chunk_step(S, inp):
    qc, kc, vc, gc = inp
    bc = jnp.cumsum(gc, axis=1)
    o_inter = jnp.einsum('bthd,bhde->bthe', qc * jnp.exp(bc), S)
    causal = jnp.tril(jnp.ones((GLA_CHUNK, GLA_CHUNK), dtype=bool))
    diff = bc[:, :, None] - bc[:, None, :]
    decay = jnp.exp(jnp.where(causal[None, :, :, None, None], diff, -jnp.inf))
    att = jnp.einsum('bthd,bshd,btshd->bhts', qc, kc, decay)
    o_intra = jnp.einsum('bhts,bshe->bthe', att, vc)
    b_last = bc[:, -1]
    S_new = jnp.exp(b_last)[..., None] * S + jnp.einsum(
        'bshd,bshe->bhde', kc * jnp.exp(b_last[:, None] - bc), vc)
    return S_new, o_inter + o_intra


def gla_mixer(h, w_in, w_a2, b_a, norm_g, w_out):
    B, L, _ = h.shape
    u = h @ w_in
    q, k, v, r, a1 = jnp.split(
        u, [GLA_DK, 2 * GLA_DK, 2 * GLA_DK + GLA_DV, 2 * GLA_DK + 2 * GLA_DV], axis=-1)
    glog = jax.nn.log_sigmoid((a1 @ w_a2 + b_a).astype(jnp.float32)) / GLA_TAU
    pad = GLA_CHUNK - N_META
    n_chunks = (L + pad) // GLA_CHUNK

    def prep(t, hd):
        t = jnp.pad(t.astype(jnp.float32), ((0, 0), (pad, 0), (0, 0)))
        return t.reshape(B, n_chunks, GLA_CHUNK, GLA_HEADS, hd).transpose(1, 0, 2, 3, 4)

    qs = prep(q, GLA_HK) * (GLA_HK ** -0.5)
    ks = prep(k, GLA_HK)
    vs = prep(v, GLA_HV)
    gs = prep(glog, GLA_HK)
    S0 = jnp.zeros((B, GLA_HEADS, GLA_HK, GLA_HV), jnp.float32)
    _, o = lax.scan(_gla_chunk_step, S0, (qs, ks, vs, gs))
    o = o.transpose(1, 0, 2, 3, 4).reshape(B, n_chunks * GLA_CHUNK, GLA_HEADS, GLA_HV)[:, pad:]
    o = o * lax.rsqrt(jnp.mean(o * o, axis=-1, keepdims=True) + RMS_EPS) * norm_g.astype(jnp.float32)
    o = o.reshape(B, L, GLA_DV).astype(h.dtype)
    return (o * jax.nn.silu(r)) @ w_out


def setup_inputs(seed: int = 0) -> dict:
    key = jax.random.key(seed)
    ks = jax.random.split(key, 20)
    nrm = jax.random.normal
    f32 = jnp.float32
    Lc, Lg = N_CONV_LAYERS, N_GLA_LAYERS
    return {
        "x": nrm(ks[0], (BATCH, SEQ, D_MODEL), f32),
        "meta": nrm(ks[1], (N_META, D_MODEL), f32),
        "conv_w_in": nrm(ks[2], (Lc, D_MODEL, CONV_IN), f32) * D_MODEL ** -0.5,
        "conv_b_in": nrm(ks[3], (Lc, CONV_IN), f32) * 0.02,
        "conv_w_dw": nrm(ks[4], (Lc, CONV_TAPS, CONV_CH), f32) * CONV_TAPS ** -0.5,
        "conv_b_dw": nrm(ks[5], (Lc, CONV_CH), f32) * 0.02,
        "conv_norm_g": 1.0 + 0.02 * nrm(ks[6], (Lc, CONV_CH), f32),
        "conv_norm_b": nrm(ks[7], (Lc, CONV_CH), f32) * 0.02,
        "conv_w_out": nrm(ks[8], (Lc, CONV_CH, D_MODEL), f32) * (CONV_CH ** -0.5 * DN_BETA),
        "conv_b_out": nrm(ks[9], (Lc, D_MODEL), f32) * 0.02,
        "gla_w_in": nrm(ks[10], (Lg, D_MODEL, GLA_IN), f32) * D_MODEL ** -0.5,
        "gla_w_a2": nrm(ks[11], (Lg, GLA_GATE_RANK, GLA_DK), f32) * GLA_GATE_RANK ** -0.5,
        "gla_b_a": nrm(ks[12], (Lg, GLA_DK), f32) * 0.02,
        "gla_norm_g": 1.0 + 0.02 * nrm(ks[13], (Lg, GLA_HV), f32),
        "gla_w_out": nrm(ks[14], (Lg, GLA_DV, D_MODEL), f32) * (GLA_DV ** -0.5 * DN_BETA),
        "post_ln_g": 1.0 + 0.02 * nrm(ks[15], (DEPTH, D_MODEL), f32),
        "post_ln_b": nrm(ks[16], (DEPTH, D_MODEL), f32) * 0.02,
    }


def reference(x, meta, conv_w_in, conv_b_in, conv_w_dw, conv_b_dw, conv_norm_g, conv_norm_b,
              conv_w_out, conv_b_out, gla_w_in, gla_w_a2, gla_b_a, gla_norm_g, gla_w_out,
              post_ln_g, post_ln_b):
    B = x.shape[0]
    h = jnp.concatenate(
        [jnp.broadcast_to(meta[None].astype(x.dtype), (B, N_META, D_MODEL)), x], axis=1)
    for i in range(DEPTH):
        j = i // N_MIXERS
        if i % N_MIXERS == 0:
            y = conformer_mixer(h, conv_w_in[j], conv_b_in[j], conv_w_dw[j], conv_b_dw[j],
                                conv_norm_g[j], conv_norm_b[j], conv_w_out[j], conv_b_out[j])
        else:
            y = gla_mixer(h, gla_w_in[j], gla_w_a2[j], gla_b_a[j], gla_norm_g[j], gla_w_out[j])
        h = layer_norm(DN_ALPHA * h + y, post_ln_g[i], post_ln_b[i])
    return h[:, N_META:]
```

```python
import functools

import jax
import jax.numpy as jnp
from jax import lax
from jax.experimental import pallas as pl
from jax.experimental.pallas import tpu as pltpu

D_MODEL = 1024
DEPTH = 4
N_META = 16
DN_ALPHA = (2 * DEPTH) ** 0.25
LN_EPS = 1e-5
RMS_EPS = 1e-6
CONV_CH = D_MODEL
CONV_TAPS = 31
GLA_HEADS = 4
GLA_DK = D_MODEL // 2
GLA_DV = D_MODEL
GLA_HK = GLA_DK // GLA_HEADS
GLA_HV = GLA_DV // GLA_HEADS
GLA_GATE_RANK = 16
GLA_TAU = 16.0

LANES = 128
CONV_HALO = 32
VMEM_LIMIT_BYTES = 56 * 1024 * 1024

F32 = jnp.float32
BF16 = jnp.bfloat16


def _layer_norm_rows(x, g, b):
    mu = jnp.mean(x, axis=-1, keepdims=True)
    xc = x - mu
    var = jnp.mean(xc * xc, axis=-1, keepdims=True)
    return xc * lax.rsqrt(var + LN_EPS) * g + b


def _silu(x):
    return x * jax.nn.sigmoid(x)


def _conv_layer_kernel(h_ref, st0_ref, w_in_ref, b_in_ref, w_dw_ref, b_dw_ref, ng_ref, nb_ref,
                       w_out_ref, b_out_ref, lg_ref, lb_ref, out_ref, st_ref,
                       glu_buf, cn_buf, *, tile, rows, crows, clanes):
    C = CONV_CH
    t = pl.program_id(1)

    @pl.when(t == 0)
    def _():
        glu_buf[0:CONV_HALO, :] = st0_ref[...]

    @pl.when(t > 0)
    def _():
        glu_buf[0:CONV_HALO, :] = glu_buf[tile:tile + CONV_HALO, :]

    def row_block(i, carry):
        r0 = pl.multiple_of(i * rows, rows)
        h = h_ref[0, pl.ds(r0, rows), :]
        x = h.astype(BF16)
        a = jnp.dot(x, w_in_ref[:, 0:C], preferred_element_type=F32) + b_in_ref[:, 0:C]
        ga = jnp.dot(x, w_in_ref[:, C:2 * C], preferred_element_type=F32) + b_in_ref[:, C:2 * C]
        glu_buf[pl.ds(CONV_HALO + r0, rows), :] = a * jax.nn.sigmoid(ga)

        base = CONV_HALO - (CONV_TAPS - 1)

        def conv_block(j, c2):
            q0 = pl.multiple_of(j * crows, crows)
            win = glu_buf.at[pl.ds(r0 + q0, crows + CONV_HALO), :]
            for l0 in range(0, C, clanes):
                acc = jnp.broadcast_to(b_dw_ref[:, l0:l0 + clanes], (crows, clanes))
                for k in range(CONV_TAPS):
                    acc = acc + w_dw_ref[k:k + 1, l0:l0 + clanes] * win[
                        base + k:base + k + crows, l0:l0 + clanes]
                cn_buf[pl.ds(q0, crows), l0:l0 + clanes] = acc
            return c2

        lax.fori_loop(0, rows // crows, conv_block, 0)

        cn = _silu(_layer_norm_rows(cn_buf[...], ng_ref[...], nb_ref[...]))
        z = jnp.dot(x, w_in_ref[:, 2 * C:3 * C], preferred_element_type=F32) + b_in_ref[:, 2 * C:3 * C]
        gated = (cn * _silu(z)).astype(BF16)
        y = jnp.dot(gated, w_out_ref[...], preferred_element_type=F32) + b_out_ref[...]
        out_ref[0, pl.ds(r0, rows), :] = _layer_norm_rows(DN_ALPHA * h + y, lg_ref[...], lb_ref[...])
        return carry

    lax.fori_loop(0, tile // rows, row_block, 0)
    st_ref[0] = glu_buf[tile:tile + CONV_HALO, :]


def _conv_layer(h, st0, w_in, b_in, w_dw, b_dw, ng, nb, w_out, b_out, lg, lb, *, tile, rows, name):
    B, L, D = h.shape
    C = CONV_CH
    crows = min(32, rows)
    clanes = 512
    const = lambda b, t: (0, 0)
    kern = functools.partial(_conv_layer_kernel, tile=tile, rows=rows, crows=crows, clanes=clanes)
    return pl.pallas_call(
        kern,
        grid=(B, L // tile),
        in_specs=[
            pl.BlockSpec((1, tile, D), lambda b, t: (b, t, 0)),
            pl.BlockSpec((CONV_HALO, C), const),
            pl.BlockSpec((D, 3 * C), const),
            pl.BlockSpec((1, 3 * C), const),
            pl.BlockSpec((CONV_HALO, C), const),
            pl.BlockSpec((1, C), const),
            pl.BlockSpec((1, C), const),
            pl.BlockSpec((1, C), const),
            pl.BlockSpec((C, D), const),
            pl.BlockSpec((1, D), const),
            pl.BlockSpec((1, D), const),
            pl.BlockSpec((1, D), const),
        ],
        out_specs=[
            pl.BlockSpec((1, tile, D), lambda b, t: (b, t, 0)),
            pl.BlockSpec((1, CONV_HALO, C), lambda b, t: (b, 0, 0)),
        ],
        out_shape=[
            jax.ShapeDtypeStruct((B, L, D), F32),
            jax.ShapeDtypeStruct((B, CONV_HALO, C), F32),
        ],
        scratch_shapes=[
            pltpu.VMEM((tile + CONV_HALO, C), F32),
            pltpu.VMEM((rows, C), F32),
        ],
        compiler_params=pltpu.CompilerParams(
            dimension_semantics=("arbitrary", "arbitrary"),
            vmem_limit_bytes=VMEM_LIMIT_BYTES),
        name=name,
    )(h, st0, w_in, b_in, w_dw, b_dw, ng, nb, w_out, b_out, lg, lb)


def _gla_layer_kernel(h_ref, s0_ref, w_in_ref, w_a1_ref, w_a2_ref, b_a_ref, ng_ref, w_out_ref,
                      lg_ref, lb_ref, out_ref, sfin_ref,
                      s_buf, q_buf, k_buf, v_buf, r_buf, g_buf, o_buf, *, tile, rows, chunk):
    DK, DV, HK, HV = GLA_DK, GLA_DV, GLA_HK, GLA_HV
    t = pl.program_id(1)

    @pl.when(t == 0)
    def _():
        s_buf[...] = s0_ref[...]

    def proj_block(i, carry):
        r0 = pl.multiple_of(i * rows, rows)
        x = h_ref[0, pl.ds(r0, rows), :].astype(BF16)
        q_buf[pl.ds(r0, rows), :] = jnp.dot(
            x, w_in_ref[:, 0:DK], preferred_element_type=F32) * (HK ** -0.5)
        k_buf[pl.ds(r0, rows), :] = jnp.dot(x, w_in_ref[:, DK:2 * DK], preferred_element_type=F32)
        v_buf[pl.ds(r0, rows), :] = jnp.dot(
            x, w_in_ref[:, 2 * DK:2 * DK + DV], preferred_element_type=F32)
        r_buf[pl.ds(r0, rows), :] = jnp.dot(
            x, w_in_ref[:, 2 * DK + DV:2 * DK + 2 * DV], preferred_element_type=F32)
        a1 = jnp.dot(x, w_a1_ref[...], preferred_element_type=F32).astype(BF16)
        xg = jnp.dot(a1, w_a2_ref[...], preferred_element_type=F32) + b_a_ref[...]
        log_sig = jnp.minimum(xg, 0.0) - jnp.log1p(jnp.exp(-jnp.abs(xg)))
        g_buf[pl.ds(r0, rows), :] = log_sig * (1.0 / GLA_TAU)
        return carry

    lax.fori_loop(0, tile // rows, proj_block, 0)

    row_id = lax.broadcasted_iota(jnp.int32, (chunk, chunk), 0)
    col_id = lax.broadcasted_iota(jnp.int32, (chunk, chunk), 1)
    causal = row_id >= col_id
    tri = jnp.where(causal, 1.0, 0.0).astype(BF16)

    def chunk_block(i, carry):
        c0 = pl.multiple_of(i * chunk, chunk)
        g = g_buf[pl.ds(c0, chunk), :]
        g1 = g.astype(BF16)
        rem = g - g1.astype(F32)
        g2 = rem.astype(BF16)
        g3 = (rem - g2.astype(F32)).astype(BF16)
        bc = (jnp.dot(tri, g1, preferred_element_type=F32)
              + jnp.dot(tri, g2, preferred_element_type=F32)
              + jnp.dot(tri, g3, preferred_element_type=F32))
        b_last = bc[chunk - 1:chunk, :]
        q = q_buf[pl.ds(c0, chunk), :]
        k = k_buf[pl.ds(c0, chunk), :]
        v = v_buf[pl.ds(c0, chunk), :].astype(BF16)
        q_dec = (q * jnp.exp(bc)).astype(BF16)
        k_inv = (k * jnp.exp(-bc)).astype(BF16)
        k_rem = (k * jnp.exp(b_last - bc)).astype(BF16)
        dec = jnp.exp(b_last)
        for hd in range(GLA_HEADS):
            ks = slice(hd * HK, (hd + 1) * HK)
            vs = slice(hd * HV, (hd + 1) * HV)
            s_t = s_buf[hd]
            att = lax.dot_general(q_dec[:, ks], k_inv[:, ks], (((1,), (1,)), ((), ())),
                                  preferred_element_type=F32)
            att = jnp.where(causal, att, 0.0).astype(BF16)
            o_inter = lax.dot_general(q_dec[:, ks], s_t.astype(BF16), (((1,), (1,)), ((), ())),
                                      preferred_element_type=F32)
            o_intra = jnp.dot(att, v[:, vs], preferred_element_type=F32)
            o_buf[pl.ds(c0, chunk), vs] = o_inter + o_intra
            upd = lax.dot_general(v[:, vs], k_rem[:, ks], (((0,), (0,)), ((), ())),
                                  preferred_element_type=F32)
            s_buf[hd] = s_t * dec[:, ks] + upd
        return carry

    lax.fori_loop(0, tile // chunk, chunk_block, 0)

    def out_block(i, carry):
        r0 = pl.multiple_of(i * rows, rows)
        parts = []
        for hd in range(GLA_HEADS):
            vs = slice(hd * HV, (hd + 1) * HV)
            o = o_buf[pl.ds(r0, rows), vs]
            o = o * lax.rsqrt(jnp.mean(o * o, axis=-1, keepdims=True) + RMS_EPS) * ng_ref[...]
            parts.append((o * _silu(r_buf[pl.ds(r0, rows), vs])).astype(BF16))
        gated = jnp.concatenate(parts, axis=-1)
        y = jnp.dot(gated, w_out_ref[...], preferred_element_type=F32)
        h = h_ref[0, pl.ds(r0, rows), :]
        out_ref[0, pl.ds(r0, rows), :] = _layer_norm_rows(DN_ALPHA * h + y, lg_ref[...], lb_ref[...])
        return carry

    lax.fori_loop(0, tile // rows, out_block, 0)
    sfin_ref[0] = s_buf[...]


def _gla_layer(h, s0, w_in, w_a1, w_a2, b_a, ng, w_out, lg, lb, *, tile, rows, chunk, name):
    B, L, D = h.shape
    DK, DV, HK, HV = GLA_DK, GLA_DV, GLA_HK, GLA_HV
    const2 = lambda b, t: (0, 0)
    kern = functools.partial(_gla_layer_kernel, tile=tile, rows=rows, chunk=chunk)
    return pl.pallas_call(
        kern,
        grid=(B, L // tile),
        in_specs=[
            pl.BlockSpec((1, tile, D), lambda b, t: (b, t, 0)),
            pl.BlockSpec((GLA_HEADS, HV, HK), lambda b, t: (0, 0, 0)),
            pl.BlockSpec((D, 2 * DK + 2 * DV), const2),
            pl.BlockSpec((D, LANES), const2),
            pl.BlockSpec((LANES, DK), const2),
            pl.BlockSpec((1, DK), const2),
            pl.BlockSpec((1, HV), const2),
            pl.BlockSpec((DV, D), const2),
            pl.BlockSpec((1, D), const2),
            pl.BlockSpec((1, D), const2),
        ],
        out_specs=[
            pl.BlockSpec((1, tile, D), lambda b, t: (b, t, 0)),
            pl.BlockSpec((1, GLA_HEADS, HV, HK), lambda b, t: (b, 0, 0, 0)),
        ],
        out_shape=[
            jax.ShapeDtypeStruct((B, L, D), F32),
            jax.ShapeDtypeStruct((B, GLA_HEADS, HV, HK), F32),
        ],
        scratch_shapes=[
            pltpu.VMEM((GLA_HEADS, HV, HK), F32),
            pltpu.VMEM((tile, DK), F32),
            pltpu.VMEM((tile, DK), F32),
            pltpu.VMEM((tile, DV), F32),
            pltpu.VMEM((tile, DV), F32),
            pltpu.VMEM((tile, DK), F32),
            pltpu.VMEM((tile, DV), F32),
        ],
        compiler_params=pltpu.CompilerParams(
            dimension_semantics=("arbitrary", "arbitrary"),
            vmem_limit_bytes=VMEM_LIMIT_BYTES),
        name=name,
    )(h, s0, w_in, w_a1, w_a2, b_a, ng, w_out, lg, lb)


MAIN_TILE = 512
MAIN_ROWS = 256
MAIN_CHUNK = 128


def kernel(x, meta, conv_w_in, conv_b_in, conv_w_dw, conv_b_dw, conv_norm_g, conv_norm_b,
           conv_w_out, conv_b_out, gla_w_in, gla_w_a2, gla_b_a, gla_norm_g, gla_w_out,
           post_ln_g, post_ln_b):
    h = x
    hm = meta[None].astype(x.dtype)
    row = lambda v: v.reshape(1, -1)
    n_qkvr = 2 * GLA_DK + 2 * GLA_DV
    for i in range(DEPTH):
        j = i // 2
        lg, lb = row(post_ln_g[i]), row(post_ln_b[i])
        if i % 2 == 0:
            w_dw = jnp.pad(conv_w_dw[j], ((0, CONV_HALO - CONV_TAPS), (0, 0)))
            args = (conv_w_in[j].astype(BF16), row(conv_b_in[j]), w_dw, row(conv_b_dw[j]),
                    row(conv_norm_g[j]), row(conv_norm_b[j]), conv_w_out[j].astype(BF16),
                    row(conv_b_out[j]), lg, lb)
            st0 = jnp.zeros((CONV_HALO, CONV_CH), F32)
            hm, st = _conv_layer(hm, st0, *args, tile=N_META, rows=N_META, name=f"conv{j}_meta")
            h, _ = _conv_layer(h, st[0], *args, tile=MAIN_TILE, rows=MAIN_ROWS, name=f"conv{j}_main")
        else:
            w_in = gla_w_in[j]
            w_a1 = jnp.pad(w_in[:, n_qkvr:], ((0, 0), (0, LANES - GLA_GATE_RANK))).astype(BF16)
            w_a2 = jnp.pad(gla_w_a2[j], ((0, LANES - GLA_GATE_RANK), (0, 0))).astype(BF16)
            args = (w_in[:, :n_qkvr].astype(BF16), w_a1, w_a2, row(gla_b_a[j]), row(gla_norm_g[j]),
                    gla_w_out[j].astype(BF16), lg, lb)
            s0 = jnp.zeros((GLA_HEADS, GLA_HV, GLA_HK), F32)
            hm, s = _gla_layer(hm, s0, *args, tile=N_META, rows=N_META, chunk=N_META,
                               name=f"gla{j}_meta")
            h, _ = _gla_layer(h, s[0], *args, tile=MAIN_TILE, rows=MAIN_ROWS, chunk=MAIN_CHUNK,
                              name=f"gla{j}_main")
    return h
```

```python
import functools

import jax
import jax.numpy as jnp
from jax import lax
from jax.experimental import pallas as pl
from jax.experimental.pallas import tpu as pltpu

D_MODEL = 1024
DEPTH = 4
N_META = 16
DN_ALPHA = (2 * DEPTH) ** 0.25
LN_EPS = 1e-5
RMS_EPS = 1e-6
CONV_CH = D_MODEL
CONV_TAPS = 31
GLA_HEADS = 4
GLA_DK = D_MODEL // 2
GLA_DV = D_MODEL
GLA_HK = GLA_DK // GLA_HEADS
GLA_HV = GLA_DV // GLA_HEADS
GLA_GATE_RANK = 16
GLA_TAU = 16.0

LANES = 128
SUBLANES = 8
CONV_HALO = 32
VMEM_LIMIT_BYTES = 56 * 1024 * 1024

F32 = jnp.float32
BF16 = jnp.bfloat16


def _layer_norm_rows(x, g, b):
    mu = jnp.mean(x, axis=-1, keepdims=True)
    xc = x - mu
    var = jnp.mean(xc * xc, axis=-1, keepdims=True)
    return xc * lax.rsqrt(var + LN_EPS) * g + b


def _silu(x):
    return x * jax.nn.sigmoid(x)


def _conv_layer_kernel(h_ref, st0_ref, w_in_ref, b_in_ref, w_dw_ref, b_dw_ref, ng_ref, nb_ref,
                       w_out_ref, b_out_ref, lg_ref, lb_ref, out_ref, st_ref,
                       glu_buf, cn_buf, *, tile, rows, clanes):
    C = CONV_CH
    t = pl.program_id(1)

    @pl.when(t == 0)
    def _():
        glu_buf[0:CONV_HALO, :] = st0_ref[...]

    @pl.when(t > 0)
    def _():
        glu_buf[0:CONV_HALO, :] = glu_buf[tile:tile + CONV_HALO, :]

    def row_block(i, carry):
        r0 = pl.multiple_of(i * rows, rows)
        h = h_ref[0, pl.ds(r0, rows), :]
        x = h.astype(BF16)
        a = jnp.dot(x, w_in_ref[:, 0:C], preferred_element_type=F32) + b_in_ref[:, 0:C]
        ga = jnp.dot(x, w_in_ref[:, C:2 * C], preferred_element_type=F32) + b_in_ref[:, C:2 * C]
        glu_buf[pl.ds(CONV_HALO + r0, rows), :] = a * jax.nn.sigmoid(ga)

        sub = lax.broadcasted_iota(jnp.int32, (SUBLANES, clanes), 0)

        def conv_lanes(l0):
            lanes = slice(l0, l0 + clanes)
            bias = jnp.broadcast_to(b_dw_ref[:, lanes], (SUBLANES, clanes))

            def conv_block(j, q_prev):
                row = pl.multiple_of(r0 + j * SUBLANES, SUBLANES)
                xs = [glu_buf[pl.ds(row + (CONV_HALO - SUBLANES * (a + 1)), SUBLANES), lanes]
                      for a in range(CONV_HALO // SUBLANES)]
                out = bias
                q_new = []
                for s in range(SUBLANES):
                    p = None
                    for a in range(CONV_HALO // SUBLANES):
                        d = SUBLANES * a + s
                        if d >= CONV_TAPS:
                            continue
                        term = w_dw_ref[SUBLANES * d:SUBLANES * (d + 1), lanes] * xs[a]
                        p = term if p is None else p + term
                    if s == 0:
                        out = out + p
                    else:
                        q = pltpu.roll(p, s, 0)
                        q_new.append(q)
                        out = out + jnp.where(sub >= s, q, q_prev[s - 1])
                cn_buf[pl.ds(pl.multiple_of(j * SUBLANES, SUBLANES), SUBLANES), lanes] = out
                return tuple(q_new)

            zero = jnp.zeros((SUBLANES, clanes), F32)
            lax.fori_loop(0, rows // SUBLANES + 1, conv_block, (zero,) * (SUBLANES - 1),
                          unroll=3 if rows >= 64 else 1)

        for l0 in range(0, C, clanes):
            conv_lanes(l0)

        cn = _silu(_layer_norm_rows(cn_buf[SUBLANES:SUBLANES + rows, :], ng_ref[...], nb_ref[...]))
        z = jnp.dot(x, w_in_ref[:, 2 * C:3 * C], preferred_element_type=F32) + b_in_ref[:, 2 * C:3 * C]
        gated = (cn * _silu(z)).astype(BF16)
        y = jnp.dot(gated, w_out_ref[...], preferred_element_type=F32) + b_out_ref[...]
        out_ref[0, pl.ds(r0, rows), :] = _layer_norm_rows(DN_ALPHA * h + y, lg_ref[...], lb_ref[...])
        return carry

    lax.fori_loop(0, tile // rows, row_block, 0)
    st_ref[0] = glu_buf[tile:tile + CONV_HALO, :]


def _conv_layer(h, st0, w_in, b_in, w_dw, b_dw, ng, nb, w_out, b_out, lg, lb, *, tile, rows, name):
    B, L, D = h.shape
    C = CONV_CH
    clanes = 256
    const = lambda b, t: (0, 0)
    kern = functools.partial(_conv_layer_kernel, tile=tile, rows=rows, clanes=clanes)
    return pl.pallas_call(
        kern,
        grid=(B, L // tile),
        in_specs=[
            pl.BlockSpec((1, tile, D), lambda b, t: (b, t, 0)),
            pl.BlockSpec((CONV_HALO, C), const),
            pl.BlockSpec((D, 3 * C), const),
            pl.BlockSpec((1, 3 * C), const),
            pl.BlockSpec((CONV_HALO * SUBLANES, C), const),
            pl.BlockSpec((1, C), const),
            pl.BlockSpec((1, C), const),
            pl.BlockSpec((1, C), const),
            pl.BlockSpec((C, D), const),
            pl.BlockSpec((1, D), const),
            pl.BlockSpec((1, D), const),
            pl.BlockSpec((1, D), const),
        ],
        out_specs=[
            pl.BlockSpec((1, tile, D), lambda b, t: (b, t, 0)),
            pl.BlockSpec((1, CONV_HALO, C), lambda b, t: (b, 0, 0)),
        ],
        out_shape=[
            jax.ShapeDtypeStruct((B, L, D), F32),
            jax.ShapeDtypeStruct((B, CONV_HALO, C), F32),
        ],
        scratch_shapes=[
            pltpu.VMEM((tile + CONV_HALO, C), F32),
            pltpu.VMEM((rows + SUBLANES, C), F32),
        ],
        compiler_params=pltpu.CompilerParams(
            dimension_semantics=("arbitrary", "arbitrary"),
            vmem_limit_bytes=VMEM_LIMIT_BYTES),
        name=name,
    )(h, st0, w_in, b_in, w_dw, b_dw, ng, nb, w_out, b_out, lg, lb)


def _gla_layer_kernel(h_ref, s0_ref, w_in_ref, w_a1_ref, w_a2_ref, b_a_ref, ng_ref, w_out_ref,
                      lg_ref, lb_ref, out_ref, sfin_ref,
                      s_buf, q_buf, k_buf, v_buf, r_buf, g_buf, o_buf, *, tile, rows, chunk):
    DK, DV, HK, HV = GLA_DK, GLA_DV, GLA_HK, GLA_HV
    t = pl.program_id(1)

    @pl.when(t == 0)
    def _():
        s_buf[...] = s0_ref[...]

    def proj_block(i, carry):
        r0 = pl.multiple_of(i * rows, rows)
        x = h_ref[0, pl.ds(r0, rows), :].astype(BF16)
        q_buf[pl.ds(r0, rows), :] = jnp.dot(
            x, w_in_ref[:, 0:DK], preferred_element_type=F32) * (HK ** -0.5)
        k_buf[pl.ds(r0, rows), :] = jnp.dot(x, w_in_ref[:, DK:2 * DK], preferred_element_type=F32)
        v_buf[pl.ds(r0, rows), :] = jnp.dot(
            x, w_in_ref[:, 2 * DK:2 * DK + DV], preferred_element_type=F32)
        r_buf[pl.ds(r0, rows), :] = jnp.dot(
            x, w_in_ref[:, 2 * DK + DV:2 * DK + 2 * DV], preferred_element_type=F32)
        a1 = jnp.dot(x, w_a1_ref[...], preferred_element_type=F32).astype(BF16)
        xg = jnp.dot(a1, w_a2_ref[...], preferred_element_type=F32) + b_a_ref[...]
        log_sig = jnp.minimum(xg, 0.0) - jnp.log1p(jnp.exp(-jnp.abs(xg)))
        g_buf[pl.ds(r0, rows), :] = log_sig * (1.0 / GLA_TAU)
        return carry

    lax.fori_loop(0, tile // rows, proj_block, 0)

    row_id = lax.broadcasted_iota(jnp.int32, (chunk, chunk), 0)
    col_id = lax.broadcasted_iota(jnp.int32, (chunk, chunk), 1)
    causal = row_id >= col_id
    tri = jnp.where(causal, 1.0, 0.0).astype(BF16)

    def chunk_block(i, carry):
        c0 = pl.multiple_of(i * chunk, chunk)
        g = g_buf[pl.ds(c0, chunk), :]
        g1 = g.astype(BF16)
        rem = g - g1.astype(F32)
        g2 = rem.astype(BF16)
        g3 = (rem - g2.astype(F32)).astype(BF16)
        bc = (jnp.dot(tri, g1, preferred_element_type=F32)
              + jnp.dot(tri, g2, preferred_element_type=F32)
              + jnp.dot(tri, g3, preferred_element_type=F32))
        b_last = bc[chunk - 1:chunk, :]
        q = q_buf[pl.ds(c0, chunk), :]
        k = k_buf[pl.ds(c0, chunk), :]
        v = v_buf[pl.ds(c0, chunk), :].astype(BF16)
        q_dec = (q * jnp.exp(bc)).astype(BF16)
        k_inv = (k * jnp.exp(-bc)).astype(BF16)
        k_rem = (k * jnp.exp(b_last - bc)).astype(BF16)
        dec = jnp.exp(b_last)
        for hd in range(GLA_HEADS):
            ks = slice(hd * HK, (hd + 1) * HK)
            vs = slice(hd * HV, (hd + 1) * HV)
            s_t = s_buf[hd]
            att = lax.dot_general(q_dec[:, ks], k_inv[:, ks], (((1,), (1,)), ((), ())),
                                  preferred_element_type=F32)
            att = jnp.where(causal, att, 0.0).astype(BF16)
            o_inter = lax.dot_general(q_dec[:, ks], s_t.astype(BF16), (((1,), (1,)), ((), ())),
                                      preferred_element_type=F32)
            o_intra = jnp.dot(att, v[:, vs], preferred_element_type=F32)
            o_buf[pl.ds(c0, chunk), vs] = o_inter + o_intra
            upd = lax.dot_general(v[:, vs], k_rem[:, ks], (((0,), (0,)), ((), ())),
                                  preferred_element_type=F32)
            s_buf[hd] = s_t * dec[:, ks] + upd
        return carry

    lax.fori_loop(0, tile // chunk, chunk_block, 0)

    def out_block(i, carry):
        r0 = pl.multiple_of(i * rows, rows)
        parts = []
        for hd in range(GLA_HEADS):
            vs = slice(hd * HV, (hd + 1) * HV)
            o = o_buf[pl.ds(r0, rows), vs]
            o = o * lax.rsqrt(jnp.mean(o * o, axis=-1, keepdims=True) + RMS_EPS) * ng_ref[...]
            parts.append((o * _silu(r_buf[pl.ds(r0, rows), vs])).astype(BF16))
        gated = jnp.concatenate(parts, axis=-1)
        y = jnp.dot(gated, w_out_ref[...], preferred_element_type=F32)
        h = h_ref[0, pl.ds(r0, rows), :]
        out_ref[0, pl.ds(r0, rows), :] = _layer_norm_rows(DN_ALPHA * h + y, lg_ref[...], lb_ref[...])
        return carry

    lax.fori_loop(0, tile // rows, out_block, 0)
    sfin_ref[0] = s_buf[...]


def _gla_layer(h, s0, w_in, w_a1, w_a2, b_a, ng, w_out, lg, lb, *, tile, rows, chunk, name):
    B, L, D = h.shape
    DK, DV, HK, HV = GLA_DK, GLA_DV, GLA_HK, GLA_HV
    const2 = lambda b, t: (0, 0)
    kern = functools.partial(_gla_layer_kernel, tile=tile, rows=rows, chunk=chunk)
    return pl.pallas_call(
        kern,
        grid=(B, L // tile),
        in_specs=[
            pl.BlockSpec((1, tile, D), lambda b, t: (b, t, 0)),
            pl.BlockSpec((GLA_HEADS, HV, HK), lambda b, t: (0, 0, 0)),
            pl.BlockSpec((D, 2 * DK + 2 * DV), const2),
            pl.BlockSpec((D, LANES), const2),
            pl.BlockSpec((LANES, DK), const2),
            pl.BlockSpec((1, DK), const2),
            pl.BlockSpec((1, HV), const2),
            pl.BlockSpec((DV, D), const2),
            pl.BlockSpec((1, D), const2),
            pl.BlockSpec((1, D), const2),
        ],
        out_specs=[
            pl.BlockSpec((1, tile, D), lambda b, t: (b, t, 0)),
            pl.BlockSpec((1, GLA_HEADS, HV, HK), lambda b, t: (b, 0, 0, 0)),
        ],
        out_shape=[
            jax.ShapeDtypeStruct((B, L, D), F32),
            jax.ShapeDtypeStruct((B, GLA_HEADS, HV, HK), F32),
        ],
        scratch_shapes=[
            pltpu.VMEM((GLA_HEADS, HV, HK), F32),
            pltpu.VMEM((tile, DK), F32),
            pltpu.VMEM((tile, DK), F32),
            pltpu.VMEM((tile, DV), F32),
            pltpu.VMEM((tile, DV), F32),
            pltpu.VMEM((tile, DK), F32),
            pltpu.VMEM((tile, DV), F32),
        ],
        compiler_params=pltpu.CompilerParams(
            dimension_semantics=("arbitrary", "arbitrary"),
            vmem_limit_bytes=VMEM_LIMIT_BYTES),
        name=name,
    )(h, s0, w_in, w_a1, w_a2, b_a, ng, w_out, lg, lb)


MAIN_TILE = 512
MAIN_ROWS = 256
MAIN_CHUNK = 128


def kernel(x, meta, conv_w_in, conv_b_in, conv_w_dw, conv_b_dw, conv_norm_g, conv_norm_b,
           conv_w_out, conv_b_out, gla_w_in, gla_w_a2, gla_b_a, gla_norm_g, gla_w_out,
           post_ln_g, post_ln_b):
    h = x
    hm = meta[None].astype(x.dtype)
    row = lambda v: v.reshape(1, -1)
    n_qkvr = 2 * GLA_DK + 2 * GLA_DV
    for i in range(DEPTH):
        j = i // 2
        lg, lb = row(post_ln_g[i]), row(post_ln_b[i])
        if i % 2 == 0:
            w_delay = jnp.flip(jnp.pad(conv_w_dw[j], ((CONV_HALO - CONV_TAPS, 0), (0, 0))), axis=0)
            w_dw = jnp.repeat(w_delay, SUBLANES, axis=0)
            args = (conv_w_in[j].astype(BF16), row(conv_b_in[j]), w_dw, row(conv_b_dw[j]),
                    row(conv_norm_g[j]), row(conv_norm_b[j]), conv_w_out[j].astype(BF16),
                    row(conv_b_out[j]), lg, lb)
            st0 = jnp.zeros((CONV_HALO, CONV_CH), F32)
            hm, st = _conv_layer(hm, st0, *args, tile=N_META, rows=N_META, name=f"conv{j}_meta")
            h, _ = _conv_layer(h, st[0], *args, tile=MAIN_TILE, rows=MAIN_ROWS, name=f"conv{j}_main")
        else:
            w_in = gla_w_in[j]
            w_a1 = jnp.pad(w_in[:, n_qkvr:], ((0, 0), (0, LANES - GLA_GATE_RANK))).astype(BF16)
            w_a2 = jnp.pad(gla_w_a2[j], ((0, LANES - GLA_GATE_RANK), (0, 0))).astype(BF16)
            args = (w_in[:, :n_qkvr].astype(BF16), w_a1, w_a2, row(gla_b_a[j]), row(gla_norm_g[j]),
                    gla_w_out[j].astype(BF16), lg, lb)
            s0 = jnp.zeros((GLA_HEADS, GLA_HV, GLA_HK), F32)
            hm, s = _gla_layer(hm, s0, *args, tile=N_META, rows=N_META, chunk=N_META,
                               name=f"gla{j}_meta")
            h, _ = _gla_layer(h, s[0], *args, tile=MAIN_TILE, rows=MAIN_ROWS, chunk=MAIN_CHUNK,
                              name=f"gla{j}_main")
    return h
```

```python
import functools

import jax
import jax.numpy as jnp
from jax import lax
from jax.experimental import pallas as pl
from jax.experimental.pallas import tpu as pltpu

D_MODEL = 1024
DEPTH = 4
N_META = 16
DN_ALPHA = (2 * DEPTH) ** 0.25
LN_EPS = 1e-5
RMS_EPS = 1e-6
CONV_CH = D_MODEL
CONV_TAPS = 31
GLA_HEADS = 4
GLA_DK = D_MODEL // 2
GLA_DV = D_MODEL
GLA_HK = GLA_DK // GLA_HEADS
GLA_HV = GLA_DV // GLA_HEADS
GLA_GATE_RANK = 16
GLA_TAU = 16.0

LANES = 128
SUBLANES = 8
CONV_HALO = 32
VMEM_LIMIT_BYTES = 56 * 1024 * 1024

F32 = jnp.float32
BF16 = jnp.bfloat16


def _layer_norm_rows(x, g, b):
    mu = jnp.mean(x, axis=-1, keepdims=True)
    xc = x - mu
    var = jnp.mean(xc * xc, axis=-1, keepdims=True)
    return xc * lax.rsqrt(var + LN_EPS) * g + b


def _sigmoid(x):
    return 0.5 * jnp.tanh(0.5 * x) + 0.5


def _silu(x):
    return x * _sigmoid(x)


def _conv_layer_kernel(h_ref, st0_ref, w_in_ref, b_in_ref, w_dw_ref, b_dw_ref, ng_ref, nb_ref,
                       w_out_ref, b_out_ref, lg_ref, lb_ref, out_ref, st_ref,
                       glu_buf, cn_buf, *, tile, rows, clanes):
    C = CONV_CH
    t = pl.program_id(1)

    @pl.when(t == 0)
    def _():
        glu_buf[0:CONV_HALO, :] = st0_ref[...]

    @pl.when(t > 0)
    def _():
        glu_buf[0:CONV_HALO, :] = glu_buf[tile:tile + CONV_HALO, :]

    def row_block(i, carry):
        r0 = pl.multiple_of(i * rows, rows)
        h = h_ref[0, pl.ds(r0, rows), :]
        x = h.astype(BF16)
        a = jnp.dot(x, w_in_ref[:, 0:C], preferred_element_type=F32) + b_in_ref[:, 0:C]
        ga = jnp.dot(x, w_in_ref[:, C:2 * C], preferred_element_type=F32) + b_in_ref[:, C:2 * C]
        glu_buf[pl.ds(CONV_HALO + r0, rows), :] = a * _sigmoid(ga)

        sub = lax.broadcasted_iota(jnp.int32, (SUBLANES, clanes), 0)

        def conv_lanes(l0):
            lanes = slice(l0, l0 + clanes)
            bias = jnp.broadcast_to(b_dw_ref[:, lanes], (SUBLANES, clanes))

            def conv_block(j, q_prev):
                row = pl.multiple_of(r0 + j * SUBLANES, SUBLANES)
                xs = [glu_buf[pl.ds(row + (CONV_HALO - SUBLANES * (a + 1)), SUBLANES), lanes]
                      for a in range(CONV_HALO // SUBLANES)]
                out = bias
                q_new = []
                for s in range(SUBLANES):
                    p = None
                    for a in range(CONV_HALO // SUBLANES):
                        d = SUBLANES * a + s
                        if d >= CONV_TAPS:
                            continue
                        term = w_dw_ref[SUBLANES * d:SUBLANES * (d + 1), lanes] * xs[a]
                        p = term if p is None else p + term
                    if s == 0:
                        out = out + p
                    else:
                        q = pltpu.roll(p, s, 0)
                        q_new.append(q)
                        out = out + jnp.where(sub >= s, q, q_prev[s - 1])
                cn_buf[pl.ds(pl.multiple_of(j * SUBLANES, SUBLANES), SUBLANES), lanes] = out
                return tuple(q_new)

            zero = jnp.zeros((SUBLANES, clanes), F32)
            lax.fori_loop(0, rows // SUBLANES + 1, conv_block, (zero,) * (SUBLANES - 1),
                          unroll=3 if rows >= 64 else 1)

        for l0 in range(0, C, clanes):
            conv_lanes(l0)

        cn = _silu(_layer_norm_rows(cn_buf[SUBLANES:SUBLANES + rows, :], ng_ref[...], nb_ref[...]))
        z = jnp.dot(x, w_in_ref[:, 2 * C:3 * C], preferred_element_type=F32) + b_in_ref[:, 2 * C:3 * C]
        gated = (cn * _silu(z)).astype(BF16)
        y = jnp.dot(gated, w_out_ref[...], preferred_element_type=F32) + b_out_ref[...]
        out_ref[0, pl.ds(r0, rows), :] = _layer_norm_rows(DN_ALPHA * h + y, lg_ref[...], lb_ref[...])
        return carry

    lax.fori_loop(0, tile // rows, row_block, 0)
    st_ref[0] = glu_buf[tile:tile + CONV_HALO, :]


def _conv_layer(h, st0, w_in, b_in, w_dw, b_dw, ng, nb, w_out, b_out, lg, lb, *, tile, rows, name):
    B, L, D = h.shape
    C = CONV_CH
    clanes = 256
    const = lambda b, t: (0, 0)
    kern = functools.partial(_conv_layer_kernel, tile=tile, rows=rows, clanes=clanes)
    return pl.pallas_call(
        kern,
        grid=(B, L // tile),
        in_specs=[
            pl.BlockSpec((1, tile, D), lambda b, t: (b, t, 0)),
            pl.BlockSpec((CONV_HALO, C), const),
            pl.BlockSpec((D, 3 * C), const),
            pl.BlockSpec((1, 3 * C), const),
            pl.BlockSpec((CONV_HALO * SUBLANES, C), const),
            pl.BlockSpec((1, C), const),
            pl.BlockSpec((1, C), const),
            pl.BlockSpec((1, C), const),
            pl.BlockSpec((C, D), const),
            pl.BlockSpec((1, D), const),
            pl.BlockSpec((1, D), const),
            pl.BlockSpec((1, D), const),
        ],
        out_specs=[
            pl.BlockSpec((1, tile, D), lambda b, t: (b, t, 0)),
            pl.BlockSpec((1, CONV_HALO, C), lambda b, t: (b, 0, 0)),
        ],
        out_shape=[
            jax.ShapeDtypeStruct((B, L, D), F32),
            jax.ShapeDtypeStruct((B, CONV_HALO, C), F32),
        ],
        scratch_shapes=[
            pltpu.VMEM((tile + CONV_HALO, C), F32),
            pltpu.VMEM((rows + SUBLANES, C), F32),
        ],
        compiler_params=pltpu.CompilerParams(
            dimension_semantics=("arbitrary", "arbitrary"),
            vmem_limit_bytes=VMEM_LIMIT_BYTES),
        name=name,
    )(h, st0, w_in, b_in, w_dw, b_dw, ng, nb, w_out, b_out, lg, lb)


def _gla_layer_kernel(h_ref, s0_ref, w_in_ref, w_a1_ref, w_a2_ref, b_a_ref, ng_ref, w_out_ref,
                      lg_ref, lb_ref, out_ref, sfin_ref,
                      s_buf, q_buf, k_buf, v_buf, r_buf, g_buf, gated_buf, *, tile, rows, chunk):
    DK, DV, HK, HV = GLA_DK, GLA_DV, GLA_HK, GLA_HV
    t = pl.program_id(1)

    @pl.when(t == 0)
    def _():
        s_buf[...] = s0_ref[...]

    def proj_block(i, carry):
        r0 = pl.multiple_of(i * rows, rows)
        x = h_ref[0, pl.ds(r0, rows), :].astype(BF16)
        q_buf[pl.ds(r0, rows), :] = jnp.dot(
            x, w_in_ref[:, 0:DK], preferred_element_type=F32) * (HK ** -0.5)
        k_buf[pl.ds(r0, rows), :] = jnp.dot(x, w_in_ref[:, DK:2 * DK], preferred_element_type=F32)
        v_buf[pl.ds(r0, rows), :] = jnp.dot(
            x, w_in_ref[:, 2 * DK:2 * DK + DV], preferred_element_type=F32)
        r_buf[pl.ds(r0, rows), :] = jnp.dot(
            x, w_in_ref[:, 2 * DK + DV:2 * DK + 2 * DV], preferred_element_type=F32)
        a1 = jnp.dot(x, w_a1_ref[...], preferred_element_type=F32).astype(BF16)
        xg = jnp.dot(a1, w_a2_ref[...], preferred_element_type=F32) + b_a_ref[...]
        log_sig = jnp.minimum(xg, 0.0) - jnp.log1p(jnp.exp(-jnp.abs(xg)))
        g_buf[pl.ds(r0, rows), :] = log_sig * (1.0 / GLA_TAU)
        return carry

    lax.fori_loop(0, tile // rows, proj_block, 0)

    row_id = lax.broadcasted_iota(jnp.int32, (chunk, chunk), 0)
    col_id = lax.broadcasted_iota(jnp.int32, (chunk, chunk), 1)
    causal = row_id >= col_id
    tri = jnp.where(causal, 1.0, 0.0).astype(BF16)

    chunks = [slice(c0, c0 + chunk) for c0 in range(0, tile, chunk)]
    heads = [(slice(hd * HK, (hd + 1) * HK), slice(hd * HV, (hd + 1) * HV))
             for hd in range(GLA_HEADS)]
    nt_dims = (((1,), (1,)), ((), ()))
    tn_dims = (((0,), (0,)), ((), ()))

    bcs = []
    for rs in chunks:
        g = g_buf[rs, :]
        g1 = g.astype(BF16)
        rem = g - g1.astype(F32)
        g2 = rem.astype(BF16)
        g3 = (rem - g2.astype(F32)).astype(BF16)
        bcs.append(jnp.dot(tri, g1, preferred_element_type=F32)
                   + jnp.dot(tri, g2, preferred_element_type=F32)
                   + jnp.dot(tri, g3, preferred_element_type=F32))

    q_decs, k_invs, k_rems, decs, vals = [], [], [], [], []
    for rs, bc in zip(chunks, bcs):
        b_last = bc[chunk - 1:chunk, :]
        k = k_buf[rs, :]
        q_decs.append((q_buf[rs, :] * jnp.exp(bc)).astype(BF16))
        k_invs.append((k * jnp.exp(-bc)).astype(BF16))
        k_rems.append((k * jnp.exp(b_last - bc)).astype(BF16))
        decs.append(jnp.exp(b_last))
        vals.append(v_buf[rs, :].astype(BF16))

    atts = [[jnp.where(causal,
                       lax.dot_general(q_decs[c][:, ks], k_invs[c][:, ks], nt_dims,
                                       preferred_element_type=F32), 0.0).astype(BF16)
             for ks, vs in heads] for c in range(len(chunks))]
    upds = [[lax.dot_general(vals[c][:, vs], k_rems[c][:, ks], tn_dims,
                             preferred_element_type=F32)
             for ks, vs in heads] for c in range(len(chunks))]
    o_intras = [[jnp.dot(atts[c][hd], vals[c][:, vs], preferred_element_type=F32)
                 for hd, (ks, vs) in enumerate(heads)] for c in range(len(chunks))]

    for c, rs in enumerate(chunks):
        for hd, (ks, vs) in enumerate(heads):
            s_t = s_buf[hd]
            o = o_intras[c][hd] + lax.dot_general(q_decs[c][:, ks], s_t.astype(BF16), nt_dims,
                                                  preferred_element_type=F32)
            s_buf[hd] = s_t * decs[c][:, ks] + upds[c][hd]
            o = o * lax.rsqrt(jnp.mean(o * o, axis=-1, keepdims=True) + RMS_EPS) * ng_ref[...]
            gated_buf[rs, vs] = (o * _silu(r_buf[rs, vs])).astype(BF16)

    for r0 in range(0, tile, rows):
        ro = slice(r0, r0 + rows)
        y = jnp.dot(gated_buf[ro, :], w_out_ref[...], preferred_element_type=F32)
        out_ref[0, ro, :] = _layer_norm_rows(
            DN_ALPHA * h_ref[0, ro, :] + y, lg_ref[...], lb_ref[...])

    sfin_ref[0] = s_buf[...]


def _gla_layer(h, s0, w_in, w_a1, w_a2, b_a, ng, w_out, lg, lb, *, tile, rows, chunk, name):
    B, L, D = h.shape
    DK, DV, HK, HV = GLA_DK, GLA_DV, GLA_HK, GLA_HV
    const2 = lambda b, t: (0, 0)
    kern = functools.partial(_gla_layer_kernel, tile=tile, rows=rows, chunk=chunk)
    return pl.pallas_call(
        kern,
        grid=(B, L // tile),
        in_specs=[
            pl.BlockSpec((1, tile, D), lambda b, t: (b, t, 0)),
            pl.BlockSpec((GLA_HEADS, HV, HK), lambda b, t: (0, 0, 0)),
            pl.BlockSpec((D, 2 * DK + 2 * DV), const2),
            pl.BlockSpec((D, LANES), const2),
            pl.BlockSpec((LANES, DK), const2),
            pl.BlockSpec((1, DK), const2),
            pl.BlockSpec((1, HV), const2),
            pl.BlockSpec((DV, D), const2),
            pl.BlockSpec((1, D), const2),
            pl.BlockSpec((1, D), const2),
        ],
        out_specs=[
            pl.BlockSpec((1, tile, D), lambda b, t: (b, t, 0)),
            pl.BlockSpec((1, GLA_HEADS, HV, HK), lambda b, t: (b, 0, 0, 0)),
        ],
        out_shape=[
            jax.ShapeDtypeStruct((B, L, D), F32),
            jax.ShapeDtypeStruct((B, GLA_HEADS, HV, HK), F32),
        ],
        scratch_shapes=[
            pltpu.VMEM((GLA_HEADS, HV, HK), F32),
            pltpu.VMEM((tile, DK), F32),
            pltpu.VMEM((tile, DK), F32),
            pltpu.VMEM((tile, DV), F32),
            pltpu.VMEM((tile, DV), F32),
            pltpu.VMEM((tile, DK), F32),
            pltpu.VMEM((tile, DV), BF16),
        ],
        compiler_params=pltpu.CompilerParams(
            dimension_semantics=("arbitrary", "arbitrary"),
            vmem_limit_bytes=VMEM_LIMIT_BYTES),
        name=name,
    )(h, s0, w_in, w_a1, w_a2, b_a, ng, w_out, lg, lb)


MAIN_TILE = 512
MAIN_ROWS = 256
MAIN_CHUNK = 128


def kernel(x, meta, conv_w_in, conv_b_in, conv_w_dw, conv_b_dw, conv_norm_g, conv_norm_b,
           conv_w_out, conv_b_out, gla_w_in, gla_w_a2, gla_b_a, gla_norm_g, gla_w_out,
           post_ln_g, post_ln_b):
    h = x
    hm = meta[None].astype(x.dtype)
    row = lambda v: v.reshape(1, -1)
    n_qkvr = 2 * GLA_DK + 2 * GLA_DV
    for i in range(DEPTH):
        j = i // 2
        lg, lb = row(post_ln_g[i]), row(post_ln_b[i])
        if i % 2 == 0:
            w_delay = jnp.flip(jnp.pad(conv_w_dw[j], ((CONV_HALO - CONV_TAPS, 0), (0, 0))), axis=0)
            w_dw = jnp.repeat(w_delay, SUBLANES, axis=0)
            args = (conv_w_in[j].astype(BF16), row(conv_b_in[j]), w_dw, row(conv_b_dw[j]),
                    row(conv_norm_g[j]), row(conv_norm_b[j]), conv_w_out[j].astype(BF16),
                    row(conv_b_out[j]), lg, lb)
            st0 = jnp.zeros((CONV_HALO, CONV_CH), F32)
            hm, st = _conv_layer(hm, st0, *args, tile=N_META, rows=N_META, name=f"conv{j}_meta")
            h, _ = _conv_layer(h, st[0], *args, tile=MAIN_TILE, rows=MAIN_ROWS, name=f"conv{j}_main")
        else:
            w_in = gla_w_in[j]
            w_a1 = jnp.pad(w_in[:, n_qkvr:], ((0, 0), (0, LANES - GLA_GATE_RANK))).astype(BF16)
            w_a2 = jnp.pad(gla_w_a2[j], ((0, LANES - GLA_GATE_RANK), (0, 0))).astype(BF16)
            args = (w_in[:, :n_qkvr].astype(BF16), w_a1, w_a2, row(gla_b_a[j]), row(gla_norm_g[j]),
                    gla_w_out[j].astype(BF16), lg, lb)
            s0 = jnp.zeros((GLA_HEADS, GLA_HV, GLA_HK), F32)
            hm, s = _gla_layer(hm, s0, *args, tile=N_META, rows=N_META, chunk=N_META,
                               name=f"gla{j}_meta")
            h, _ = _gla_layer(h, s[0], *args, tile=MAIN_TILE, rows=MAIN_ROWS, chunk=MAIN_CHUNK,
                              name=f"gla{j}_main")
    return h
```

```python
import functools

import jax
import jax.numpy as jnp
from jax import lax
from jax.experimental import pallas as pl
from jax.experimental.pallas import tpu as pltpu

D_MODEL = 1024
DEPTH = 4
N_META = 16
DN_ALPHA = (2 * DEPTH) ** 0.25
LN_EPS = 1e-5
RMS_EPS = 1e-6
CONV_CH = D_MODEL
CONV_TAPS = 31
GLA_HEADS = 4
GLA_DK = D_MODEL // 2
GLA_DV = D_MODEL
GLA_HK = GLA_DK // GLA_HEADS
GLA_HV = GLA_DV // GLA_HEADS
GLA_GATE_RANK = 16
GLA_TAU = 16.0

LANES = 128
SUBLANES = 8
CONV_HALO = 32
VMEM_LIMIT_BYTES = 56 * 1024 * 1024

F32 = jnp.float32
BF16 = jnp.bfloat16


def _layer_norm_rows(x, g, b):
    mu = jnp.mean(x, axis=-1, keepdims=True)
    xc = x - mu
    var = jnp.mean(xc * xc, axis=-1, keepdims=True)
    return xc * lax.rsqrt(var + LN_EPS) * g + b


def _sigmoid(x):
    return 0.5 * jnp.tanh(0.5 * x) + 0.5


def _silu(x):
    return x * _sigmoid(x)


def _conv_layer_kernel(h_ref, st0_ref, w_in_ref, b_in_ref, w_dw_ref, b_dw_ref, ng_ref, nb_ref,
                       w_out_ref, b_out_ref, lg_ref, lb_ref, out_ref, st_ref,
                       glu_buf, cn_buf, *, tile, rows, clanes):
    C = CONV_CH
    t = pl.program_id(1)

    @pl.when(t == 0)
    def _():
        glu_buf[0:CONV_HALO, :] = st0_ref[...]

    @pl.when(t > 0)
    def _():
        glu_buf[0:CONV_HALO, :] = glu_buf[tile:tile + CONV_HALO, :]

    def row_block(i, carry):
        r0 = pl.multiple_of(i * rows, rows)
        h = h_ref[0, pl.ds(r0, rows), :]
        x = h.astype(BF16)
        a = jnp.dot(x, w_in_ref[:, 0:C], preferred_element_type=F32) + b_in_ref[:, 0:C]
        ga = jnp.dot(x, w_in_ref[:, C:2 * C], preferred_element_type=F32) + b_in_ref[:, C:2 * C]
        glu_buf[pl.ds(CONV_HALO + r0, rows), :] = a * _sigmoid(ga)

        sub = lax.broadcasted_iota(jnp.int32, (SUBLANES, clanes), 0)

        def conv_lanes(l0):
            lanes = slice(l0, l0 + clanes)
            bias = jnp.broadcast_to(b_dw_ref[:, lanes], (SUBLANES, clanes))

            def conv_block(j, q_prev):
                row = pl.multiple_of(r0 + j * SUBLANES, SUBLANES)
                xs = [glu_buf[pl.ds(row + (CONV_HALO - SUBLANES * (a + 1)), SUBLANES), lanes]
                      for a in range(CONV_HALO // SUBLANES)]
                out = bias
                q_new = []
                for s in range(SUBLANES):
                    p = None
                    for a in range(CONV_HALO // SUBLANES):
                        d = SUBLANES * a + s
                        if d >= CONV_TAPS:
                            continue
                        term = w_dw_ref[SUBLANES * d:SUBLANES * (d + 1), lanes] * xs[a]
                        p = term if p is None else p + term
                    if s == 0:
                        out = out + p
                    else:
                        q = pltpu.roll(p, s, 0)
                        q_new.append(q)
                        out = out + jnp.where(sub >= s, q, q_prev[s - 1])
                cn_buf[pl.ds(pl.multiple_of(j * SUBLANES, SUBLANES), SUBLANES), lanes] = out
                return tuple(q_new)

            zero = jnp.zeros((SUBLANES, clanes), F32)
            lax.fori_loop(0, rows // SUBLANES + 1, conv_block, (zero,) * (SUBLANES - 1),
                          unroll=3 if rows >= 64 else 1)

        for l0 in range(0, C, clanes):
            conv_lanes(l0)

        cn = _silu(_layer_norm_rows(cn_buf[SUBLANES:SUBLANES + rows, :], ng_ref[...], nb_ref[...]))
        z = jnp.dot(x, w_in_ref[:, 2 * C:3 * C], preferred_element_type=F32) + b_in_ref[:, 2 * C:3 * C]
        gated = (cn * _silu(z)).astype(BF16)
        y = jnp.dot(gated, w_out_ref[...], preferred_element_type=F32) + b_out_ref[...]
        out_ref[0, pl.ds(r0, rows), :] = _layer_norm_rows(DN_ALPHA * h + y, lg_ref[...], lb_ref[...])
        return carry

    lax.fori_loop(0, tile // rows, row_block, 0)
    st_ref[0] = glu_buf[tile:tile + CONV_HALO, :]


def _conv_layer(h, st0, w_in, b_in, w_dw, b_dw, ng, nb, w_out, b_out, lg, lb, *, tile, rows, name):
    B, L, D = h.shape
    C = CONV_CH
    clanes = 256
    const = lambda b, t: (0, 0)
    kern = functools.partial(_conv_layer_kernel, tile=tile, rows=rows, clanes=clanes)
    return pl.pallas_call(
        kern,
        grid=(B, L // tile),
        in_specs=[
            pl.BlockSpec((1, tile, D), lambda b, t: (b, t, 0)),
            pl.BlockSpec((CONV_HALO, C), const),
            pl.BlockSpec((D, 3 * C), const),
            pl.BlockSpec((1, 3 * C), const),
            pl.BlockSpec((CONV_HALO * SUBLANES, C), const),
            pl.BlockSpec((1, C), const),
            pl.BlockSpec((1, C), const),
            pl.BlockSpec((1, C), const),
            pl.BlockSpec((C, D), const),
            pl.BlockSpec((1, D), const),
            pl.BlockSpec((1, D), const),
            pl.BlockSpec((1, D), const),
        ],
        out_specs=[
            pl.BlockSpec((1, tile, D), lambda b, t: (b, t, 0)),
            pl.BlockSpec((1, CONV_HALO, C), lambda b, t: (b, 0, 0)),
        ],
        out_shape=[
            jax.ShapeDtypeStruct((B, L, D), F32),
            jax.ShapeDtypeStruct((B, CONV_HALO, C), F32),
        ],
        scratch_shapes=[
            pltpu.VMEM((tile + CONV_HALO, C), F32),
            pltpu.VMEM((rows + SUBLANES, C), F32),
        ],
        compiler_params=pltpu.CompilerParams(
            dimension_semantics=("arbitrary", "arbitrary"),
            vmem_limit_bytes=VMEM_LIMIT_BYTES),
        name=name,
    )(h, st0, w_in, b_in, w_dw, b_dw, ng, nb, w_out, b_out, lg, lb)


def _gla_layer_kernel(h_ref, s0_ref, w_in_ref, w_a1_ref, w_a2_ref, b_a_ref, ng_ref, w_out_ref,
                      lg_ref, lb_ref, out_ref, sfin_ref,
                      s_buf, q_buf, k_buf, v_buf, r_buf, g_buf, gated_buf, *, tile, rows, chunk):
    DK, DV, HK, HV = GLA_DK, GLA_DV, GLA_HK, GLA_HV
    t = pl.program_id(1)

    @pl.when(t == 0)
    def _():
        s_buf[...] = s0_ref[...]

    x = h_ref[0].astype(BF16)
    a1 = jnp.dot(x, w_a1_ref[...], preferred_element_type=F32).astype(BF16)
    q_buf[...] = jnp.dot(x, w_in_ref[:, 0:DK], preferred_element_type=F32) * (HK ** -0.5)
    k_buf[...] = jnp.dot(x, w_in_ref[:, DK:2 * DK], preferred_element_type=F32)
    xg = jnp.dot(a1, w_a2_ref[...], preferred_element_type=F32) + b_a_ref[...]
    v_buf[...] = jnp.dot(x, w_in_ref[:, 2 * DK:2 * DK + DV],
                         preferred_element_type=F32).astype(BF16)
    r_buf[...] = jnp.dot(x, w_in_ref[:, 2 * DK + DV:2 * DK + 2 * DV], preferred_element_type=F32)
    log_sig = jnp.minimum(xg, 0.0) - jnp.log1p(jnp.exp(-jnp.abs(xg)))
    g_buf[...] = log_sig * (1.0 / GLA_TAU)

    row_id = lax.broadcasted_iota(jnp.int32, (chunk, chunk), 0)
    col_id = lax.broadcasted_iota(jnp.int32, (chunk, chunk), 1)
    causal = row_id >= col_id
    tri = jnp.where(causal, 1.0, 0.0).astype(BF16)

    chunks = [slice(c0, c0 + chunk) for c0 in range(0, tile, chunk)]
    heads = [(slice(hd * HK, (hd + 1) * HK), slice(hd * HV, (hd + 1) * HV))
             for hd in range(GLA_HEADS)]
    nt_dims = (((1,), (1,)), ((), ()))
    tn_dims = (((0,), (0,)), ((), ()))
    mid = chunk // 2 - 1

    bcs = []
    for rs in chunks:
        g = g_buf[rs, :]
        g1 = g.astype(BF16)
        rem = g - g1.astype(F32)
        g2 = rem.astype(BF16)
        g3 = (rem - g2.astype(F32)).astype(BF16)
        bcs.append(jnp.dot(tri, g1, preferred_element_type=F32)
                   + jnp.dot(tri, g2, preferred_element_type=F32)
                   + jnp.dot(tri, g3, preferred_element_type=F32))

    q_decs, q_mids, k_mids, k_rems, decs = [], [], [], [], []
    for rs, bc in zip(chunks, bcs):
        b_last = bc[chunk - 1:chunk, :]
        b_mid = bc[mid:mid + 1, :]
        q = q_buf[rs, :]
        k = k_buf[rs, :]
        q_decs.append((q * jnp.exp(bc)).astype(BF16))
        q_mids.append((q * jnp.exp(bc - b_mid)).astype(BF16))
        k_mids.append((k * jnp.exp(b_mid - bc)).astype(BF16))
        k_rems.append((k * jnp.exp(b_last - bc)).astype(BF16))
        decs.append(jnp.exp(b_last))

    atts = [[jnp.where(causal,
                       lax.dot_general(q_mids[c][:, ks], k_mids[c][:, ks], nt_dims,
                                       preferred_element_type=F32), 0.0).astype(BF16)
             for ks, vs in heads] for c in range(len(chunks))]
    upds = [[lax.dot_general(v_buf[rs, vs], k_rems[c][:, ks], tn_dims,
                             preferred_element_type=F32)
             for ks, vs in heads] for c, rs in enumerate(chunks)]
    o_intras = [[jnp.dot(atts[c][hd], v_buf[rs, vs], preferred_element_type=F32)
                 for hd, (ks, vs) in enumerate(heads)] for c, rs in enumerate(chunks)]

    for c, rs in enumerate(chunks):
        for hd, (ks, vs) in enumerate(heads):
            s_t = s_buf[hd]
            o = o_intras[c][hd] + lax.dot_general(q_decs[c][:, ks], s_t.astype(BF16), nt_dims,
                                                  preferred_element_type=F32)
            s_buf[hd] = s_t * decs[c][:, ks] + upds[c][hd]
            o = o * lax.rsqrt(jnp.mean(o * o, axis=-1, keepdims=True) + RMS_EPS) * ng_ref[...]
            gated_buf[rs, vs] = (o * _silu(r_buf[rs, vs])).astype(BF16)

    for r0 in range(0, tile, rows):
        ro = slice(r0, r0 + rows)
        y = jnp.dot(gated_buf[ro, :], w_out_ref[...], preferred_element_type=F32)
        out_ref[0, ro, :] = _layer_norm_rows(
            DN_ALPHA * h_ref[0, ro, :] + y, lg_ref[...], lb_ref[...])

    sfin_ref[0] = s_buf[...]


def _gla_layer(h, s0, w_in, w_a1, w_a2, b_a, ng, w_out, lg, lb, *, tile, rows, chunk, name):
    B, L, D = h.shape
    DK, DV, HK, HV = GLA_DK, GLA_DV, GLA_HK, GLA_HV
    const2 = lambda b, t: (0, 0)
    kern = functools.partial(_gla_layer_kernel, tile=tile, rows=rows, chunk=chunk)
    return pl.pallas_call(
        kern,
        grid=(B, L // tile),
        in_specs=[
            pl.BlockSpec((1, tile, D), lambda b, t: (b, t, 0)),
            pl.BlockSpec((GLA_HEADS, HV, HK), lambda b, t: (0, 0, 0)),
            pl.BlockSpec((D, 2 * DK + 2 * DV), const2),
            pl.BlockSpec((D, LANES), const2),
            pl.BlockSpec((LANES, DK), const2),
            pl.BlockSpec((1, DK), const2),
            pl.BlockSpec((1, HV), const2),
            pl.BlockSpec((DV, D), const2),
            pl.BlockSpec((1, D), const2),
            pl.BlockSpec((1, D), const2),
        ],
        out_specs=[
            pl.BlockSpec((1, tile, D), lambda b, t: (b, t, 0)),
            pl.BlockSpec((1, GLA_HEADS, HV, HK), lambda b, t: (b, 0, 0, 0)),
        ],
        out_shape=[
            jax.ShapeDtypeStruct((B, L, D), F32),
            jax.ShapeDtypeStruct((B, GLA_HEADS, HV, HK), F32),
        ],
        scratch_shapes=[
            pltpu.VMEM((GLA_HEADS, HV, HK), F32),
            pltpu.VMEM((tile, DK), F32),
            pltpu.VMEM((tile, DK), F32),
            pltpu.VMEM((tile, DV), BF16),
            pltpu.VMEM((tile, DV), F32),
            pltpu.VMEM((tile, DK), F32),
            pltpu.VMEM((tile, DV), BF16),
        ],
        compiler_params=pltpu.CompilerParams(
            dimension_semantics=("arbitrary", "arbitrary"),
            vmem_limit_bytes=VMEM_LIMIT_BYTES),
        name=name,
    )(h, s0, w_in, w_a1, w_a2, b_a, ng, w_out, lg, lb)


MAIN_TILE = 512
MAIN_ROWS = 256
MAIN_CHUNK = 128


def kernel(x, meta, conv_w_in, conv_b_in, conv_w_dw, conv_b_dw, conv_norm_g, conv_norm_b,
           conv_w_out, conv_b_out, gla_w_in, gla_w_a2, gla_b_a, gla_norm_g, gla_w_out,
           post_ln_g, post_ln_b):
    h = x
    hm = meta[None].astype(x.dtype)
    row = lambda v: v.reshape(1, -1)
    n_qkvr = 2 * GLA_DK + 2 * GLA_DV
    for i in range(DEPTH):
        j = i // 2
        lg, lb = row(post_ln_g[i]), row(post_ln_b[i])
        if i % 2 == 0:
            w_delay = jnp.flip(jnp.pad(conv_w_dw[j], ((CONV_HALO - CONV_TAPS, 0), (0, 0))), axis=0)
            w_dw = jnp.repeat(w_delay, SUBLANES, axis=0)
            args = (conv_w_in[j].astype(BF16), row(conv_b_in[j]), w_dw, row(conv_b_dw[j]),
                    row(conv_norm_g[j]), row(conv_norm_b[j]), conv_w_out[j].astype(BF16),
                    row(conv_b_out[j]), lg, lb)
            st0 = jnp.zeros((CONV_HALO, CONV_CH), F32)
            hm, st = _conv_layer(hm, st0, *args, tile=N_META, rows=N_META, name=f"conv{j}_meta")
            h, _ = _conv_layer(h, st[0], *args, tile=MAIN_TILE, rows=MAIN_ROWS, name=f"conv{j}_main")
        else:
            w_in = gla_w_in[j]
            w_a1 = jnp.pad(w_in[:, n_qkvr:], ((0, 0), (0, LANES - GLA_GATE_RANK))).astype(BF16)
            w_a2 = jnp.pad(gla_w_a2[j], ((0, LANES - GLA_GATE_RANK), (0, 0))).astype(BF16)
            args = (w_in[:, :n_qkvr].astype(BF16), w_a1, w_a2, row(gla_b_a[j]), row(gla_norm_g[j]),
                    gla_w_out[j].astype(BF16), lg, lb)
            s0 = jnp.zeros((GLA_HEADS, GLA_HV, GLA_HK), F32)
            hm, s = _gla_layer(hm, s0, *args, tile=N_META, rows=N_META, chunk=N_META,
                               name=f"gla{j}_meta")
            h, _ = _gla_layer(h, s[0], *args, tile=MAIN_TILE, rows=MAIN_ROWS, chunk=MAIN_CHUNK,
                              name=f"gla{j}_main")
    return h
```

```python
import functools

import jax
import jax.numpy as jnp
from jax import lax
from jax.experimental import pallas as pl
from jax.experimental.pallas import tpu as pltpu

D_MODEL = 1024
DEPTH = 4
N_META = 16
DN_ALPHA = (2 * DEPTH) ** 0.25
LN_EPS = 1e-5
RMS_EPS = 1e-6
CONV_CH = D_MODEL
CONV_TAPS = 31
GLA_HEADS = 4
GLA_DK = D_MODEL // 2
GLA_DV = D_MODEL
GLA_HK = GLA_DK // GLA_HEADS
GLA_HV = GLA_DV // GLA_HEADS
GLA_GATE_RANK = 16
GLA_TAU = 16.0

LANES = 128
SUBLANES = 8
CONV_HALO = 32
VMEM_LIMIT_BYTES = 56 * 1024 * 1024

F32 = jnp.float32
BF16 = jnp.bfloat16


def _layer_norm_rows(x, g, b):
    mu = jnp.mean(x, axis=-1, keepdims=True)
    xc = x - mu
    var = jnp.mean(xc * xc, axis=-1, keepdims=True)
    return xc * lax.rsqrt(var + LN_EPS) * g + b


def _sigmoid(x):
    return 0.5 * jnp.tanh(0.5 * x) + 0.5


def _silu(x):
    return x * _sigmoid(x)


def _conv_layer_kernel(h_ref, st0_ref, w_in_ref, b_in_ref, w_dw_ref, b_dw_ref, ng_ref, nb_ref,
                       w_out_ref, b_out_ref, lg_ref, lb_ref, out_ref, st_ref,
                       glu_buf, cn_buf, *, tile, rows, clanes, n_inline):
    C = CONV_CH
    t = pl.program_id(1)

    @pl.when(t == 0)
    def _():
        glu_buf[0:CONV_HALO, :] = st0_ref[...]

    @pl.when(t > 0)
    def _():
        glu_buf[0:CONV_HALO, :] = glu_buf[tile:tile + CONV_HALO, :]

    def glu_matmuls(i):
        x = h_ref[0, i * rows:(i + 1) * rows, :].astype(BF16)
        a = jnp.dot(x, w_in_ref[:, 0:C], preferred_element_type=F32)
        ga = jnp.dot(x, w_in_ref[:, C:2 * C], preferred_element_type=F32)
        return a, ga

    def glu_store(i, a, ga):
        glu_buf[CONV_HALO + i * rows:CONV_HALO + (i + 1) * rows, :] = (
            (a + b_in_ref[:, 0:C]) * _sigmoid(ga + b_in_ref[:, C:2 * C]))

    sub = lax.broadcasted_iota(jnp.int32, (SUBLANES, clanes), 0)

    def conv_block(row, j8, lanes, bias, q_prev):
        xs = [glu_buf[pl.ds(row + (CONV_HALO - SUBLANES * (a + 1)), SUBLANES), lanes]
              for a in range(CONV_HALO // SUBLANES)]
        out = bias
        q_new = []
        for s in range(SUBLANES):
            p = None
            for a in range(CONV_HALO // SUBLANES):
                d = SUBLANES * a + s
                if d >= CONV_TAPS:
                    continue
                term = w_dw_ref[SUBLANES * d:SUBLANES * (d + 1), lanes] * xs[a]
                p = term if p is None else p + term
            if s == 0:
                out = out + p
            else:
                q = pltpu.roll(p, s, 0)
                q_new.append(q)
                out = out + jnp.where(sub >= s, q, q_prev[s - 1])
        cn_buf[pl.ds(j8, SUBLANES), lanes] = out
        return tuple(q_new)

    def conv_lanes(i, l0, inline):
        lanes = slice(l0, l0 + clanes)
        bias = jnp.broadcast_to(b_dw_ref[:, lanes], (SUBLANES, clanes))
        carry = (jnp.zeros((SUBLANES, clanes), F32),) * (SUBLANES - 1)
        n_steps = rows // SUBLANES + 1
        if inline:
            for j in range(n_steps):
                carry = conv_block(i * rows + j * SUBLANES, j * SUBLANES, lanes, bias, carry)
        else:
            def step(j, q_prev):
                j8 = pl.multiple_of(j * SUBLANES, SUBLANES)
                return conv_block(i * rows + j8, j8, lanes, bias, q_prev)
            lax.fori_loop(0, n_steps, step, carry, unroll=3 if rows >= 64 else 1)

    n_blocks = tile // rows
    lane_groups = list(range(0, C, clanes))
    n_loop = len(lane_groups) - n_inline
    glu_store(0, *glu_matmuls(0))
    for i in range(n_blocks):
        for l0 in lane_groups[:n_loop]:
            conv_lanes(i, l0, inline=False)
        for l0 in lane_groups[n_loop:]:
            conv_lanes(i, l0, inline=True)
        rs = slice(i * rows, (i + 1) * rows)
        h = h_ref[0, rs, :]
        z = jnp.dot(h.astype(BF16), w_in_ref[:, 2 * C:3 * C], preferred_element_type=F32)
        nxt = glu_matmuls(i + 1) if i + 1 < n_blocks else None
        cn = _silu(_layer_norm_rows(cn_buf[SUBLANES:SUBLANES + rows, :], ng_ref[...], nb_ref[...]))
        gated = (cn * _silu(z + b_in_ref[:, 2 * C:3 * C])).astype(BF16)
        y = jnp.dot(gated, w_out_ref[...], preferred_element_type=F32) + b_out_ref[...]
        if nxt is not None:
            glu_store(i + 1, *nxt)
        out_ref[0, rs, :] = _layer_norm_rows(DN_ALPHA * h + y, lg_ref[...], lb_ref[...])
    st_ref[0] = glu_buf[tile:tile + CONV_HALO, :]


def _conv_layer(h, st0, w_in, b_in, w_dw, b_dw, ng, nb, w_out, b_out, lg, lb, *, tile, rows, name):
    B, L, D = h.shape
    C = CONV_CH
    clanes = 256
    n_inline = 1 if tile > rows else 0
    const = lambda b, t: (0, 0)
    kern = functools.partial(_conv_layer_kernel, tile=tile, rows=rows, clanes=clanes,
                             n_inline=n_inline)
    return pl.pallas_call(
        kern,
        grid=(B, L // tile),
        in_specs=[
            pl.BlockSpec((1, tile, D), lambda b, t: (b, t, 0)),
            pl.BlockSpec((CONV_HALO, C), const),
            pl.BlockSpec((D, 3 * C), const),
            pl.BlockSpec((1, 3 * C), const),
            pl.BlockSpec((CONV_HALO * SUBLANES, C), const),
            pl.BlockSpec((1, C), const),
            pl.BlockSpec((1, C), const),
            pl.BlockSpec((1, C), const),
            pl.BlockSpec((C, D), const),
            pl.BlockSpec((1, D), const),
            pl.BlockSpec((1, D), const),
            pl.BlockSpec((1, D), const),
        ],
        out_specs=[
            pl.BlockSpec((1, tile, D), lambda b, t: (b, t, 0)),
            pl.BlockSpec((1, CONV_HALO, C), lambda b, t: (b, 0, 0)),
        ],
        out_shape=[
            jax.ShapeDtypeStruct((B, L, D), F32),
            jax.ShapeDtypeStruct((B, CONV_HALO, C), F32),
        ],
        scratch_shapes=[
            pltpu.VMEM((tile + CONV_HALO, C), F32),
            pltpu.VMEM((rows + SUBLANES, C), F32),
        ],
        compiler_params=pltpu.CompilerParams(
            dimension_semantics=("arbitrary", "arbitrary"),
            vmem_limit_bytes=VMEM_LIMIT_BYTES),
        name=name,
    )(h, st0, w_in, b_in, w_dw, b_dw, ng, nb, w_out, b_out, lg, lb)


def _gla_layer_kernel(h_ref, s0_ref, w_in_ref, w_a1_ref, w_a2_ref, b_a_ref, ng_ref, w_out_ref,
                      lg_ref, lb_ref, out_ref, sfin_ref,
                      s_buf, q_buf, k_buf, v_buf, r_buf, g_buf, gated_buf, *, tile, rows, chunk):
    DK, DV, HK, HV = GLA_DK, GLA_DV, GLA_HK, GLA_HV
    t = pl.program_id(1)

    @pl.when(t == 0)
    def _():
        s_buf[...] = s0_ref[...]

    x = h_ref[0].astype(BF16)
    a1 = jnp.dot(x, w_a1_ref[...], preferred_element_type=F32).astype(BF16)
    q_buf[...] = jnp.dot(x, w_in_ref[:, 0:DK], preferred_element_type=F32) * (HK ** -0.5)
    k_buf[...] = jnp.dot(x, w_in_ref[:, DK:2 * DK], preferred_element_type=F32)
    xg = jnp.dot(a1, w_a2_ref[...], preferred_element_type=F32) + b_a_ref[...]
    v_buf[...] = jnp.dot(x, w_in_ref[:, 2 * DK:2 * DK + DV],
                         preferred_element_type=F32).astype(BF16)
    r_buf[...] = jnp.dot(x, w_in_ref[:, 2 * DK + DV:2 * DK + 2 * DV], preferred_element_type=F32)
    log_sig = jnp.minimum(xg, 0.0) - jnp.log1p(jnp.exp(-jnp.abs(xg)))
    g_buf[...] = log_sig * (1.0 / GLA_TAU)

    row_id = lax.broadcasted_iota(jnp.int32, (chunk, chunk), 0)
    col_id = lax.broadcasted_iota(jnp.int32, (chunk, chunk), 1)
    causal = row_id >= col_id
    tri = jnp.where(causal, 1.0, 0.0).astype(BF16)

    chunks = [slice(c0, c0 + chunk) for c0 in range(0, tile, chunk)]
    heads = [(slice(hd * HK, (hd + 1) * HK), slice(hd * HV, (hd + 1) * HV))
             for hd in range(GLA_HEADS)]
    nt_dims = (((1,), (1,)), ((), ()))
    tn_dims = (((0,), (0,)), ((), ()))
    mid = chunk // 2 - 1

    bcs = []
    for rs in chunks:
        g = g_buf[rs, :]
        g1 = g.astype(BF16)
        rem = g - g1.astype(F32)
        g2 = rem.astype(BF16)
        g3 = (rem - g2.astype(F32)).astype(BF16)
        bcs.append(jnp.dot(tri, g1, preferred_element_type=F32)
                   + jnp.dot(tri, g2, preferred_element_type=F32)
                   + jnp.dot(tri, g3, preferred_element_type=F32))

    q_decs, q_mids, k_mids, k_rems, decs = [], [], [], [], []
    for rs, bc in zip(chunks, bcs):
        b_last = bc[chunk - 1:chunk, :]
        b_mid = bc[mid:mid + 1, :]
        q = q_buf[rs, :]
        k = k_buf[rs, :]
        q_decs.append((q * jnp.exp(bc)).astype(BF16))
        q_mids.append((q * jnp.exp(bc - b_mid)).astype(BF16))
        k_mids.append((k * jnp.exp(b_mid - bc)).astype(BF16))
        k_rems.append((k * jnp.exp(b_last - bc)).astype(BF16))
        decs.append(jnp.exp(b_last))

    atts = [[jnp.where(causal,
                       lax.dot_general(q_mids[c][:, ks], k_mids[c][:, ks], nt_dims,
                                       preferred_element_type=F32), 0.0).astype(BF16)
             for ks, vs in heads] for c in range(len(chunks))]
    upds = [[lax.dot_general(v_buf[rs, vs], k_rems[c][:, ks], tn_dims,
                             preferred_element_type=F32)
             for ks, vs in heads] for c, rs in enumerate(chunks)]
    o_intras = [[jnp.dot(atts[c][hd], v_buf[rs, vs], preferred_element_type=F32)
                 for hd, (ks, vs) in enumerate(heads)] for c, rs in enumerate(chunks)]

    for c, rs in enumerate(chunks):
        for hd, (ks, vs) in enumerate(heads):
            s_t = s_buf[hd]
            o = o_intras[c][hd] + lax.dot_general(q_decs[c][:, ks], s_t.astype(BF16), nt_dims,
                                                  preferred_element_type=F32)
            s_buf[hd] = s_t * decs[c][:, ks] + upds[c][hd]
            o = o * lax.rsqrt(jnp.mean(o * o, axis=-1, keepdims=True) + RMS_EPS) * ng_ref[...]
            gated_buf[rs, vs] = (o * _silu(r_buf[rs, vs])).astype(BF16)

    for r0 in range(0, tile, rows):
        ro = slice(r0, r0 + rows)
        y = jnp.dot(gated_buf[ro, :], w_out_ref[...], preferred_element_type=F32)
        out_ref[0, ro, :] = _layer_norm_rows(
            DN_ALPHA * h_ref[0, ro, :] + y, lg_ref[...], lb_ref[...])

    sfin_ref[0] = s_buf[...]


def _gla_layer(h, s0, w_in, w_a1, w_a2, b_a, ng, w_out, lg, lb, *, tile, rows, chunk, name):
    B, L, D = h.shape
    DK, DV, HK, HV = GLA_DK, GLA_DV, GLA_HK, GLA_HV
    const2 = lambda b, t: (0, 0)
    kern = functools.partial(_gla_layer_kernel, tile=tile, rows=rows, chunk=chunk)
    return pl.pallas_call(
        kern,
        grid=(B, L // tile),
        in_specs=[
            pl.BlockSpec((1, tile, D), lambda b, t: (b, t, 0)),
            pl.BlockSpec((GLA_HEADS, HV, HK), lambda b, t: (0, 0, 0)),
            pl.BlockSpec((D, 2 * DK + 2 * DV), const2),
            pl.BlockSpec((D, LANES), const2),
            pl.BlockSpec((LANES, DK), const2),
            pl.BlockSpec((1, DK), const2),
            pl.BlockSpec((1, HV), const2),
            pl.BlockSpec((DV, D), const2),
            pl.BlockSpec((1, D), const2),
            pl.BlockSpec((1, D), const2),
        ],
        out_specs=[
            pl.BlockSpec((1, tile, D), lambda b, t: (b, t, 0)),
            pl.BlockSpec((1, GLA_HEADS, HV, HK), lambda b, t: (b, 0, 0, 0)),
        ],
        out_shape=[
            jax.ShapeDtypeStruct((B, L, D), F32),
            jax.ShapeDtypeStruct((B, GLA_HEADS, HV, HK), F32),
        ],
        scratch_shapes=[
            pltpu.VMEM((GLA_HEADS, HV, HK), F32),
            pltpu.VMEM((tile, DK), F32),
            pltpu.VMEM((tile, DK), F32),
            pltpu.VMEM((tile, DV), BF16),
            pltpu.VMEM((tile, DV), F32),
            pltpu.VMEM((tile, DK), F32),
            pltpu.VMEM((tile, DV), BF16),
        ],
        compiler_params=pltpu.CompilerParams(
            dimension_semantics=("arbitrary", "arbitrary"),
            vmem_limit_bytes=VMEM_LIMIT_BYTES),
        name=name,
    )(h, s0, w_in, w_a1, w_a2, b_a, ng, w_out, lg, lb)


MAIN_TILE = 512
MAIN_ROWS = 256
MAIN_CHUNK = 128
CONV_TILE = 1024


def kernel(x, meta, conv_w_in, conv_b_in, conv_w_dw, conv_b_dw, conv_norm_g, conv_norm_b,
           conv_w_out, conv_b_out, gla_w_in, gla_w_a2, gla_b_a, gla_norm_g, gla_w_out,
           post_ln_g, post_ln_b):
    h = x
    hm = meta[None].astype(x.dtype)
    row = lambda v: v.reshape(1, -1)
    n_qkvr = 2 * GLA_DK + 2 * GLA_DV
    for i in range(DEPTH):
        j = i // 2
        lg, lb = row(post_ln_g[i]), row(post_ln_b[i])
        if i % 2 == 0:
            w_delay = jnp.flip(jnp.pad(conv_w_dw[j], ((CONV_HALO - CONV_TAPS, 0), (0, 0))), axis=0)
            w_dw = jnp.repeat(w_delay, SUBLANES, axis=0)
            args = (conv_w_in[j].astype(BF16), row(conv_b_in[j]), w_dw, row(conv_b_dw[j]),
                    row(conv_norm_g[j]), row(conv_norm_b[j]), conv_w_out[j].astype(BF16),
                    row(conv_b_out[j]), lg, lb)
            st0 = jnp.zeros((CONV_HALO, CONV_CH), F32)
            hm, st = _conv_layer(hm, st0, *args, tile=N_META, rows=N_META, name=f"conv{j}_meta")
            h, _ = _conv_layer(h, st[0], *args, tile=CONV_TILE, rows=MAIN_ROWS, name=f"conv{j}_main")
        else:
            w_in = gla_w_in[j]
            w_a1 = jnp.pad(w_in[:, n_qkvr:], ((0, 0), (0, LANES - GLA_GATE_RANK))).astype(BF16)
            w_a2 = jnp.pad(gla_w_a2[j], ((0, LANES - GLA_GATE_RANK), (0, 0))).astype(BF16)
            args = (w_in[:, :n_qkvr].astype(BF16), w_a1, w_a2, row(gla_b_a[j]), row(gla_norm_g[j]),
                    gla_w_out[j].astype(BF16), lg, lb)
            s0 = jnp.zeros((GLA_HEADS, GLA_HV, GLA_HK), F32)
            hm, s = _gla_layer(hm, s0, *args, tile=N_META, rows=N_META, chunk=N_META,
                               name=f"gla{j}_meta")
            h, _ = _gla_layer(h, s[0], *args, tile=MAIN_TILE, rows=MAIN_ROWS, chunk=MAIN_CHUNK,
                              name=f"gla{j}_main")
    return h
```

```python
import functools

import jax
import jax.numpy as jnp
from jax import lax
from jax.experimental import pallas as pl
from jax.experimental.pallas import tpu as pltpu

D_MODEL = 1024
DEPTH = 4
N_META = 16
DN_ALPHA = (2 * DEPTH) ** 0.25
LN_EPS = 1e-5
RMS_EPS = 1e-6
CONV_CH = D_MODEL
CONV_TAPS = 31
GLA_HEADS = 4
GLA_DK = D_MODEL // 2
GLA_DV = D_MODEL
GLA_HK = GLA_DK // GLA_HEADS
GLA_HV = GLA_DV // GLA_HEADS
GLA_GATE_RANK = 16
GLA_TAU = 16.0

LANES = 128
SUBLANES = 8
CONV_HALO = 32
VMEM_LIMIT_BYTES = 56 * 1024 * 1024

F32 = jnp.float32
BF16 = jnp.bfloat16


def _layer_norm_rows(x, g, b):
    mu = jnp.mean(x, axis=-1, keepdims=True)
    xc = x - mu
    var = jnp.mean(xc * xc, axis=-1, keepdims=True)
    return xc * lax.rsqrt(var + LN_EPS) * g + b


def _sigmoid(x):
    return 0.5 * jnp.tanh(0.5 * x) + 0.5


def _silu(x):
    return x * _sigmoid(x)


def _conv_layer_kernel(h_ref, st0_ref, w_in_ref, b_in_ref, w_dw_ref, b_dw_ref, ng_ref, nb_ref,
                       w_out_ref, b_out_ref, lg_ref, lb_ref, out_ref, st_ref,
                       glu_buf, cn_buf, *, tile, rows, clanes, n_inline):
    C = CONV_CH
    t = pl.program_id(1)

    @pl.when(t == 0)
    def _():
        glu_buf[0:CONV_HALO, :] = st0_ref[...]

    @pl.when(t > 0)
    def _():
        glu_buf[0:CONV_HALO, :] = glu_buf[tile:tile + CONV_HALO, :]

    def glu_matmuls(i):
        x = h_ref[0, i * rows:(i + 1) * rows, :].astype(BF16)
        a = jnp.dot(x, w_in_ref[:, 0:C], preferred_element_type=F32)
        ga = jnp.dot(x, w_in_ref[:, C:2 * C], preferred_element_type=F32)
        return a, ga

    def glu_store(i, a, ga):
        glu_buf[CONV_HALO + i * rows:CONV_HALO + (i + 1) * rows, :] = (
            (a + b_in_ref[:, 0:C]) * _sigmoid(ga + b_in_ref[:, C:2 * C]))

    sub = lax.broadcasted_iota(jnp.int32, (SUBLANES, clanes), 0)

    def conv_block(row, j8, lanes, bias, q_prev):
        xs = [glu_buf[pl.ds(row + (CONV_HALO - SUBLANES * (a + 1)), SUBLANES), lanes]
              for a in range(CONV_HALO // SUBLANES)]
        out = bias
        q_new = []
        for s in range(SUBLANES):
            p = None
            for a in range(CONV_HALO // SUBLANES):
                d = SUBLANES * a + s
                if d >= CONV_TAPS:
                    continue
                term = w_dw_ref[SUBLANES * d:SUBLANES * (d + 1), lanes] * xs[a]
                p = term if p is None else p + term
            if s == 0:
                out = out + p
            else:
                q = pltpu.roll(p, s, 0)
                q_new.append(q)
                out = out + jnp.where(sub >= s, q, q_prev[s - 1])
        cn_buf[pl.ds(j8, SUBLANES), lanes] = out
        return tuple(q_new)

    def conv_lanes(i, l0, inline):
        lanes = slice(l0, l0 + clanes)
        bias = jnp.broadcast_to(b_dw_ref[:, lanes], (SUBLANES, clanes))
        carry = (jnp.zeros((SUBLANES, clanes), F32),) * (SUBLANES - 1)
        n_steps = rows // SUBLANES + 1
        if inline:
            for j in range(n_steps):
                carry = conv_block(i * rows + j * SUBLANES, j * SUBLANES, lanes, bias, carry)
        else:
            def step(j, q_prev):
                j8 = pl.multiple_of(j * SUBLANES, SUBLANES)
                return conv_block(i * rows + j8, j8, lanes, bias, q_prev)
            lax.fori_loop(0, n_steps, step, carry, unroll=3 if rows >= 64 else 1)

    n_blocks = tile // rows
    lane_groups = list(range(0, C, clanes))
    n_loop = len(lane_groups) - n_inline
    glu_store(0, *glu_matmuls(0))
    for i in range(n_blocks):
        for l0 in lane_groups[:n_loop]:
            conv_lanes(i, l0, inline=False)
        for l0 in lane_groups[n_loop:]:
            conv_lanes(i, l0, inline=True)
        rs = slice(i * rows, (i + 1) * rows)
        h = h_ref[0, rs, :]
        z = jnp.dot(h.astype(BF16), w_in_ref[:, 2 * C:3 * C], preferred_element_type=F32)
        nxt = glu_matmuls(i + 1) if i + 1 < n_blocks else None
        cn = _silu(_layer_norm_rows(cn_buf[SUBLANES:SUBLANES + rows, :], ng_ref[...], nb_ref[...]))
        gated = (cn * _silu(z + b_in_ref[:, 2 * C:3 * C])).astype(BF16)
        y = jnp.dot(gated, w_out_ref[...], preferred_element_type=F32) + b_out_ref[...]
        if nxt is not None:
            glu_store(i + 1, *nxt)
        out_ref[0, rs, :] = _layer_norm_rows(DN_ALPHA * h + y, lg_ref[...], lb_ref[...])
    st_ref[0] = glu_buf[tile:tile + CONV_HALO, :]


def _conv_layer(h, st0, w_in, b_in, w_dw, b_dw, ng, nb, w_out, b_out, lg, lb, *, tile, rows, name):
    B, L, D = h.shape
    C = CONV_CH
    clanes = 256
    n_inline = 1 if tile > rows else 0
    const = lambda b, t: (0, 0)
    kern = functools.partial(_conv_layer_kernel, tile=tile, rows=rows, clanes=clanes,
                             n_inline=n_inline)
    return pl.pallas_call(
        kern,
        grid=(B, L // tile),
        in_specs=[
            pl.BlockSpec((1, tile, D), lambda b, t: (b, t, 0)),
            pl.BlockSpec((CONV_HALO, C), const),
            pl.BlockSpec((D, 3 * C), const),
            pl.BlockSpec((1, 3 * C), const),
            pl.BlockSpec((CONV_HALO * SUBLANES, C), const),
            pl.BlockSpec((1, C), const),
            pl.BlockSpec((1, C), const),
            pl.BlockSpec((1, C), const),
            pl.BlockSpec((C, D), const),
            pl.BlockSpec((1, D), const),
            pl.BlockSpec((1, D), const),
            pl.BlockSpec((1, D), const),
        ],
        out_specs=[
            pl.BlockSpec((1, tile, D), lambda b, t: (b, t, 0)),
            pl.BlockSpec((1, CONV_HALO, C), lambda b, t: (b, 0, 0)),
        ],
        out_shape=[
            jax.ShapeDtypeStruct((B, L, D), F32),
            jax.ShapeDtypeStruct((B, CONV_HALO, C), F32),
        ],
        scratch_shapes=[
            pltpu.VMEM((tile + CONV_HALO, C), F32),
            pltpu.VMEM((rows + SUBLANES, C), F32),
        ],
        compiler_params=pltpu.CompilerParams(
            dimension_semantics=("arbitrary", "arbitrary"),
            vmem_limit_bytes=VMEM_LIMIT_BYTES),
        name=name,
    )(h, st0, w_in, b_in, w_dw, b_dw, ng, nb, w_out, b_out, lg, lb)


def _gla_layer_kernel(h_ref, s0_ref, w_in_ref, w_a1_ref, w_a2_ref, b_a_ref, ng_ref, w_out_ref,
                      lg_ref, lb_ref, out_ref, sfin_ref,
                      s_buf, q_buf, k_buf, v_buf, r_buf, g_buf, gated_buf, *, tile, rows, chunk):
    DK, DV, HK, HV = GLA_DK, GLA_DV, GLA_HK, GLA_HV
    t = pl.program_id(1)

    @pl.when(t == 0)
    def _():
        s_buf[...] = s0_ref[...]

    x = h_ref[0].astype(BF16)
    a1 = jnp.dot(x, w_a1_ref[...], preferred_element_type=F32).astype(BF16)
    q_buf[...] = jnp.dot(x, w_in_ref[:, 0:DK], preferred_element_type=F32) * (HK ** -0.5)
    k_buf[...] = jnp.dot(x, w_in_ref[:, DK:2 * DK], preferred_element_type=F32)
    xg = jnp.dot(a1, w_a2_ref[...], preferred_element_type=F32) + b_a_ref[...]
    v_buf[...] = jnp.dot(x, w_in_ref[:, 2 * DK:2 * DK + DV],
                         preferred_element_type=F32).astype(BF16)
    r_buf[...] = jnp.dot(x, w_in_ref[:, 2 * DK + DV:2 * DK + 2 * DV], preferred_element_type=F32)
    log_sig = jnp.minimum(xg, 0.0) - jnp.log1p(jnp.exp(-jnp.abs(xg)))
    g_buf[...] = log_sig * (1.0 / GLA_TAU)

    row_id = lax.broadcasted_iota(jnp.int32, (chunk, chunk), 0)
    col_id = lax.broadcasted_iota(jnp.int32, (chunk, chunk), 1)
    causal = row_id >= col_id
    tri = jnp.where(causal, 1.0, 0.0).astype(BF16)

    chunks = [slice(c0, c0 + chunk) for c0 in range(0, tile, chunk)]
    heads = [(slice(hd * HK, (hd + 1) * HK), slice(hd * HV, (hd + 1) * HV))
             for hd in range(GLA_HEADS)]
    nt_dims = (((1,), (1,)), ((), ()))
    tn_dims = (((0,), (0,)), ((), ()))
    mid = chunk // 2 - 1

    bcs = []
    for rs in chunks:
        g = g_buf[rs, :]
        g1 = g.astype(BF16)
        rem = g - g1.astype(F32)
        g2 = rem.astype(BF16)
        g3 = (rem - g2.astype(F32)).astype(BF16)
        bcs.append(jnp.dot(tri, g1, preferred_element_type=F32)
                   + jnp.dot(tri, g2, preferred_element_type=F32)
                   + jnp.dot(tri, g3, preferred_element_type=F32))

    q_decs, q_mids, k_mids, k_rems, decs = [], [], [], [], []
    for rs, bc in zip(chunks, bcs):
        b_last = bc[chunk - 1:chunk, :]
        b_mid = bc[mid:mid + 1, :]
        q = q_buf[rs, :]
        k = k_buf[rs, :]
        q_decs.append((q * jnp.exp(bc)).astype(BF16))
        q_mids.append((q * jnp.exp(bc - b_mid)).astype(BF16))
        k_mids.append((k * jnp.exp(b_mid - bc)).astype(BF16))
        k_rems.append((k * jnp.exp(b_last - bc)).astype(BF16))
        decs.append(jnp.exp(b_last))

    atts = [[jnp.where(causal,
                       lax.dot_general(q_mids[c][:, ks], k_mids[c][:, ks], nt_dims,
                                       preferred_element_type=F32), 0.0).astype(BF16)
             for ks, vs in heads] for c in range(len(chunks))]
    upds = [[lax.dot_general(v_buf[rs, vs], k_rems[c][:, ks], tn_dims,
                             preferred_element_type=F32)
             for ks, vs in heads] for c, rs in enumerate(chunks)]
    o_intras = [[jnp.dot(atts[c][hd], v_buf[rs, vs], preferred_element_type=F32)
                 for hd, (ks, vs) in enumerate(heads)] for c, rs in enumerate(chunks)]

    for c, rs in enumerate(chunks):
        for hd, (ks, vs) in enumerate(heads):
            s_t = s_buf[hd]
            o = o_intras[c][hd] + lax.dot_general(q_decs[c][:, ks], s_t.astype(BF16), nt_dims,
                                                  preferred_element_type=F32)
            s_buf[hd] = s_t * decs[c][:, ks] + upds[c][hd]
            o = o * lax.rsqrt(jnp.mean(o * o, axis=-1, keepdims=True) + RMS_EPS) * ng_ref[...]
            gated_buf[rs, vs] = (o * _silu(r_buf[rs, vs])).astype(BF16)

    for r0 in range(0, tile, rows):
        ro = slice(r0, r0 + rows)
        y = jnp.dot(gated_buf[ro, :], w_out_ref[...], preferred_element_type=F32)
        out_ref[0, ro, :] = _layer_norm_rows(
            DN_ALPHA * h_ref[0, ro, :] + y, lg_ref[...], lb_ref[...])

    sfin_ref[0] = s_buf[...]


def _gla_layer(h, s0, w_in, w_a1, w_a2, b_a, ng, w_out, lg, lb, *, tile, rows, chunk, name):
    B, L, D = h.shape
    DK, DV, HK, HV = GLA_DK, GLA_DV, GLA_HK, GLA_HV
    const2 = lambda b, t: (0, 0)
    kern = functools.partial(_gla_layer_kernel, tile=tile, rows=rows, chunk=chunk)
    return pl.pallas_call(
        kern,
        grid=(B, L // tile),
        in_specs=[
            pl.BlockSpec((1, tile, D), lambda b, t: (b, t, 0)),
            pl.BlockSpec((GLA_HEADS, HV, HK), lambda b, t: (0, 0, 0)),
            pl.BlockSpec((D, 2 * DK + 2 * DV), const2),
            pl.BlockSpec((D, LANES), const2),
            pl.BlockSpec((LANES, DK), const2),
            pl.BlockSpec((1, DK), const2),
            pl.BlockSpec((1, HV), const2),
            pl.BlockSpec((DV, D), const2),
            pl.BlockSpec((1, D), const2),
            pl.BlockSpec((1, D), const2),
        ],
        out_specs=[
            pl.BlockSpec((1, tile, D), lambda b, t: (b, t, 0)),
            pl.BlockSpec((1, GLA_HEADS, HV, HK), lambda b, t: (b, 0, 0, 0)),
        ],
        out_shape=[
            jax.ShapeDtypeStruct((B, L, D), F32),
            jax.ShapeDtypeStruct((B, GLA_HEADS, HV, HK), F32),
        ],
        scratch_shapes=[
            pltpu.VMEM((GLA_HEADS, HV, HK), F32),
            pltpu.VMEM((tile, DK), F32),
            pltpu.VMEM((tile, DK), F32),
            pltpu.VMEM((tile, DV), BF16),
            pltpu.VMEM((tile, DV), F32),
            pltpu.VMEM((tile, DK), F32),
            pltpu.VMEM((tile, DV), BF16),
        ],
        compiler_params=pltpu.CompilerParams(
            dimension_semantics=("arbitrary", "arbitrary"),
            vmem_limit_bytes=VMEM_LIMIT_BYTES),
        name=name,
    )(h, s0, w_in, w_a1, w_a2, b_a, ng, w_out, lg, lb)


MAIN_TILE = 1024
MAIN_ROWS = 256
MAIN_CHUNK = 128
CONV_TILE = 1024


def kernel(x, meta, conv_w_in, conv_b_in, conv_w_dw, conv_b_dw, conv_norm_g, conv_norm_b,
           conv_w_out, conv_b_out, gla_w_in, gla_w_a2, gla_b_a, gla_norm_g, gla_w_out,
           post_ln_g, post_ln_b):
    h = x
    hm = meta[None].astype(x.dtype)
    row = lambda v: v.reshape(1, -1)
    n_qkvr = 2 * GLA_DK + 2 * GLA_DV
    for i in range(DEPTH):
        j = i // 2
        lg, lb = row(post_ln_g[i]), row(post_ln_b[i])
        if i % 2 == 0:
            w_delay = jnp.flip(jnp.pad(conv_w_dw[j], ((CONV_HALO - CONV_TAPS, 0), (0, 0))), axis=0)
            w_dw = jnp.repeat(w_delay, SUBLANES, axis=0)
            args = (conv_w_in[j].astype(BF16), row(conv_b_in[j]), w_dw, row(conv_b_dw[j]),
                    row(conv_norm_g[j]), row(conv_norm_b[j]), conv_w_out[j].astype(BF16),
                    row(conv_b_out[j]), lg, lb)
            st0 = jnp.zeros((CONV_HALO, CONV_CH), F32)
            hm, st = _conv_layer(hm, st0, *args, tile=N_META, rows=N_META, name=f"conv{j}_meta")
            h, _ = _conv_layer(h, st[0], *args, tile=CONV_TILE, rows=MAIN_ROWS, name=f"conv{j}_main")
        else:
            w_in = gla_w_in[j]
            w_a1 = jnp.pad(w_in[:, n_qkvr:], ((0, 0), (0, LANES - GLA_GATE_RANK))).astype(BF16)
            w_a2 = jnp.pad(gla_w_a2[j], ((0, LANES - GLA_GATE_RANK), (0, 0))).astype(BF16)
            args = (w_in[:, :n_qkvr].astype(BF16), w_a1, w_a2, row(gla_b_a[j]), row(gla_norm_g[j]),
                    gla_w_out[j].astype(BF16), lg, lb)
            s0 = jnp.zeros((GLA_HEADS, GLA_HV, GLA_HK), F32)
            hm, s = _gla_layer(hm, s0, *args, tile=N_META, rows=N_META, chunk=N_META,
                               name=f"gla{j}_meta")
            h, _ = _gla_layer(h, s[0], *args, tile=MAIN_TILE, rows=MAIN_ROWS, chunk=MAIN_CHUNK,
                              name=f"gla{j}_main")
    return h
```

```python
import functools

import jax
import jax.numpy as jnp
from jax import lax
from jax.experimental import pallas as pl
from jax.experimental.pallas import tpu as pltpu

D_MODEL = 1024
DEPTH = 4
N_META = 16
DN_ALPHA = (2 * DEPTH) ** 0.25
LN_EPS = 1e-5
RMS_EPS = 1e-6
CONV_CH = D_MODEL
CONV_TAPS = 31
GLA_HEADS = 4
GLA_DK = D_MODEL // 2
GLA_DV = D_MODEL
GLA_HK = GLA_DK // GLA_HEADS
GLA_HV = GLA_DV // GLA_HEADS
GLA_GATE_RANK = 16
GLA_TAU = 16.0

LANES = 128
SUBLANES = 8
CONV_HALO = 32
VMEM_LIMIT_BYTES = 56 * 1024 * 1024

F32 = jnp.float32
BF16 = jnp.bfloat16


def _layer_norm_rows(x, g, b):
    mu = jnp.mean(x, axis=-1, keepdims=True)
    xc = x - mu
    var = jnp.mean(xc * xc, axis=-1, keepdims=True)
    return xc * lax.rsqrt(var + LN_EPS) * g + b


def _sigmoid(x):
    return 0.5 * jnp.tanh(0.5 * x) + 0.5


def _silu(x):
    return x * _sigmoid(x)


def _conv_layer_kernel(h_ref, st0_ref, w_in_ref, b_in_ref, w_dw_ref, b_dw_ref, ng_ref, nb_ref,
                       w_out_ref, b_out_ref, lg_ref, lb_ref, out_ref, st_ref,
                       glu_buf, cn_buf, *, tile, rows, clanes, n_inline):
    C = CONV_CH
    t = pl.program_id(1)

    @pl.when(t == 0)
    def _():
        glu_buf[0:CONV_HALO, :] = st0_ref[...]

    @pl.when(t > 0)
    def _():
        glu_buf[0:CONV_HALO, :] = glu_buf[tile:tile + CONV_HALO, :]

    def glu_matmuls(i):
        x = h_ref[0, i * rows:(i + 1) * rows, :].astype(BF16)
        a = jnp.dot(x, w_in_ref[:, 0:C], preferred_element_type=F32)
        ga = jnp.dot(x, w_in_ref[:, C:2 * C], preferred_element_type=F32)
        return a, ga

    def glu_store(i, a, ga):
        glu_buf[CONV_HALO + i * rows:CONV_HALO + (i + 1) * rows, :] = (
            (a + b_in_ref[:, 0:C]) * _sigmoid(ga + b_in_ref[:, C:2 * C]))

    sub = lax.broadcasted_iota(jnp.int32, (SUBLANES, clanes), 0)

    def conv_block(row, j8, lanes, bias, q_prev):
        xs = [glu_buf[pl.ds(row + (CONV_HALO - SUBLANES * (a + 1)), SUBLANES), lanes]
              for a in range(CONV_HALO // SUBLANES)]
        out = bias
        q_new = []
        for s in range(SUBLANES):
            p = None
            for a in range(CONV_HALO // SUBLANES):
                d = SUBLANES * a + s
                if d >= CONV_TAPS:
                    continue
                term = w_dw_ref[SUBLANES * d:SUBLANES * (d + 1), lanes] * xs[a]
                p = term if p is None else p + term
            if s == 0:
                out = out + p
            else:
                q = pltpu.roll(p, s, 0)
                q_new.append(q)
                out = out + jnp.where(sub >= s, q, q_prev[s - 1])
        cn_buf[pl.ds(j8, SUBLANES), lanes] = out
        return tuple(q_new)

    def conv_lanes(i, l0, inline):
        lanes = slice(l0, l0 + clanes)
        bias = jnp.broadcast_to(b_dw_ref[:, lanes], (SUBLANES, clanes))
        carry = (jnp.zeros((SUBLANES, clanes), F32),) * (SUBLANES - 1)
        n_steps = rows // SUBLANES + 1
        if inline:
            for j in range(n_steps):
                carry = conv_block(i * rows + j * SUBLANES, j * SUBLANES, lanes, bias, carry)
        else:
            def step(j, q_prev):
                j8 = pl.multiple_of(j * SUBLANES, SUBLANES)
                return conv_block(i * rows + j8, j8, lanes, bias, q_prev)
            lax.fori_loop(0, n_steps, step, carry, unroll=11 if rows >= 64 else 1)

    n_blocks = tile // rows
    lane_groups = list(range(0, C, clanes))
    n_loop = len(lane_groups) - n_inline
    glu_store(0, *glu_matmuls(0))
    for i in range(n_blocks):
        for l0 in lane_groups[:n_loop]:
            conv_lanes(i, l0, inline=False)
        for l0 in lane_groups[n_loop:]:
            conv_lanes(i, l0, inline=True)
        rs = slice(i * rows, (i + 1) * rows)
        h = h_ref[0, rs, :]
        z = jnp.dot(h.astype(BF16), w_in_ref[:, 2 * C:3 * C], preferred_element_type=F32)
        nxt = glu_matmuls(i + 1) if i + 1 < n_blocks else None
        cn = _silu(_layer_norm_rows(cn_buf[SUBLANES:SUBLANES + rows, :], ng_ref[...], nb_ref[...]))
        gated = (cn * _silu(z + b_in_ref[:, 2 * C:3 * C])).astype(BF16)
        y = jnp.dot(gated, w_out_ref[...], preferred_element_type=F32) + b_out_ref[...]
        if nxt is not None:
            glu_store(i + 1, *nxt)
        out_ref[0, rs, :] = _layer_norm_rows(DN_ALPHA * h + y, lg_ref[...], lb_ref[...])
    st_ref[0] = glu_buf[tile:tile + CONV_HALO, :]


def _conv_layer(h, st0, w_in, b_in, w_dw, b_dw, ng, nb, w_out, b_out, lg, lb, *, tile, rows, name):
    B, L, D = h.shape
    C = CONV_CH
    clanes = 256
    n_inline = 1 if tile > rows else 0
    const = lambda b, t: (0, 0)
    kern = functools.partial(_conv_layer_kernel, tile=tile, rows=rows, clanes=clanes,
                             n_inline=n_inline)
    return pl.pallas_call(
        kern,
        grid=(B, L // tile),
        in_specs=[
            pl.BlockSpec((1, tile, D), lambda b, t: (b, t, 0)),
            pl.BlockSpec((CONV_HALO, C), const),
            pl.BlockSpec((D, 3 * C), const),
            pl.BlockSpec((1, 3 * C), const),
            pl.BlockSpec((CONV_HALO * SUBLANES, C), const),
            pl.BlockSpec((1, C), const),
            pl.BlockSpec((1, C), const),
            pl.BlockSpec((1, C), const),
            pl.BlockSpec((C, D), const),
            pl.BlockSpec((1, D), const),
            pl.BlockSpec((1, D), const),
            pl.BlockSpec((1, D), const),
        ],
        out_specs=[
            pl.BlockSpec((1, tile, D), lambda b, t: (b, t, 0)),
            pl.BlockSpec((1, CONV_HALO, C), lambda b, t: (b, 0, 0)),
        ],
        out_shape=[
            jax.ShapeDtypeStruct((B, L, D), F32),
            jax.ShapeDtypeStruct((B, CONV_HALO, C), F32),
        ],
        scratch_shapes=[
            pltpu.VMEM((tile + CONV_HALO, C), F32),
            pltpu.VMEM((rows + SUBLANES, C), F32),
        ],
        compiler_params=pltpu.CompilerParams(
            dimension_semantics=("arbitrary", "arbitrary"),
            vmem_limit_bytes=VMEM_LIMIT_BYTES),
        name=name,
    )(h, st0, w_in, b_in, w_dw, b_dw, ng, nb, w_out, b_out, lg, lb)


def _gla_layer_kernel(h_ref, s0_ref, w_in_ref, w_a1_ref, w_a2_ref, b_a_ref, ng_ref, w_out_ref,
                      lg_ref, lb_ref, out_ref, sfin_ref,
                      s_buf, q_buf, k_buf, v_buf, r_buf, g_buf, gated_buf, *, tile, rows, chunk):
    DK, DV, HK, HV = GLA_DK, GLA_DV, GLA_HK, GLA_HV
    t = pl.program_id(1)

    @pl.when(t == 0)
    def _():
        s_buf[...] = s0_ref[...]

    x = h_ref[0].astype(BF16)
    a1 = jnp.dot(x, w_a1_ref[...], preferred_element_type=F32).astype(BF16)
    q_buf[...] = jnp.dot(x, w_in_ref[:, 0:DK], preferred_element_type=F32) * (HK ** -0.5)
    k_buf[...] = jnp.dot(x, w_in_ref[:, DK:2 * DK], preferred_element_type=F32)
    xg = jnp.dot(a1, w_a2_ref[...], preferred_element_type=F32) + b_a_ref[...]
    v_buf[...] = jnp.dot(x, w_in_ref[:, 2 * DK:2 * DK + DV],
                         preferred_element_type=F32).astype(BF16)
    r_buf[...] = jnp.dot(x, w_in_ref[:, 2 * DK + DV:2 * DK + 2 * DV], preferred_element_type=F32)
    log_sig = jnp.minimum(xg, 0.0) - jnp.log1p(jnp.exp(-jnp.abs(xg)))
    g_buf[...] = log_sig * (1.0 / GLA_TAU)

    row_id = lax.broadcasted_iota(jnp.int32, (chunk, chunk), 0)
    col_id = lax.broadcasted_iota(jnp.int32, (chunk, chunk), 1)
    causal = row_id >= col_id
    tri = jnp.where(causal, 1.0, 0.0).astype(BF16)

    chunks = [slice(c0, c0 + chunk) for c0 in range(0, tile, chunk)]
    heads = [(slice(hd * HK, (hd + 1) * HK), slice(hd * HV, (hd + 1) * HV))
             for hd in range(GLA_HEADS)]
    nt_dims = (((1,), (1,)), ((), ()))
    tn_dims = (((0,), (0,)), ((), ()))
    mid = chunk // 2 - 1

    bcs = []
    for rs in chunks:
        g = g_buf[rs, :]
        g1 = g.astype(BF16)
        rem = g - g1.astype(F32)
        g2 = rem.astype(BF16)
        g3 = (rem - g2.astype(F32)).astype(BF16)
        bcs.append(jnp.dot(tri, g1, preferred_element_type=F32)
                   + jnp.dot(tri, g2, preferred_element_type=F32)
                   + jnp.dot(tri, g3, preferred_element_type=F32))

    q_decs, q_mids, k_mids, k_rems, decs = [], [], [], [], []
    for rs, bc in zip(chunks, bcs):
        b_last = bc[chunk - 1:chunk, :]
        b_mid = bc[mid:mid + 1, :]
        q = q_buf[rs, :]
        k = k_buf[rs, :]
        q_decs.append((q * jnp.exp(bc)).astype(BF16))
        q_mids.append((q * jnp.exp(bc - b_mid)).astype(BF16))
        k_mids.append((k * jnp.exp(b_mid - bc)).astype(BF16))
        k_rems.append((k * jnp.exp(b_last - bc)).astype(BF16))
        decs.append(jnp.exp(b_last))

    atts = [[jnp.where(causal,
                       lax.dot_general(q_mids[c][:, ks], k_mids[c][:, ks], nt_dims,
                                       preferred_element_type=F32), 0.0).astype(BF16)
             for ks, vs in heads] for c in range(len(chunks))]
    upds = [[lax.dot_general(v_buf[rs, vs], k_rems[c][:, ks], tn_dims,
                             preferred_element_type=F32)
             for ks, vs in heads] for c, rs in enumerate(chunks)]
    o_intras = [[jnp.dot(atts[c][hd], v_buf[rs, vs], preferred_element_type=F32)
                 for hd, (ks, vs) in enumerate(heads)] for c, rs in enumerate(chunks)]

    for c, rs in enumerate(chunks):
        for hd, (ks, vs) in enumerate(heads):
            s_t = s_buf[hd]
            o = o_intras[c][hd] + lax.dot_general(q_decs[c][:, ks], s_t.astype(BF16), nt_dims,
                                                  preferred_element_type=F32)
            s_buf[hd] = s_t * decs[c][:, ks] + upds[c][hd]
            o = o * lax.rsqrt(jnp.mean(o * o, axis=-1, keepdims=True) + RMS_EPS) * ng_ref[...]
            gated_buf[rs, vs] = (o * _silu(r_buf[rs, vs])).astype(BF16)

    for r0 in range(0, tile, rows):
        ro = slice(r0, r0 + rows)
        y = jnp.dot(gated_buf[ro, :], w_out_ref[...], preferred_element_type=F32)
        out_ref[0, ro, :] = _layer_norm_rows(
            DN_ALPHA * h_ref[0, ro, :] + y, lg_ref[...], lb_ref[...])

    sfin_ref[0] = s_buf[...]


def _gla_layer(h, s0, w_in, w_a1, w_a2, b_a, ng, w_out, lg, lb, *, tile, rows, chunk, name):
    B, L, D = h.shape
    DK, DV, HK, HV = GLA_DK, GLA_DV, GLA_HK, GLA_HV
    const2 = lambda b, t: (0, 0)
    kern = functools.partial(_gla_layer_kernel, tile=tile, rows=rows, chunk=chunk)
    return pl.pallas_call(
        kern,
        grid=(B, L // tile),
        in_specs=[
            pl.BlockSpec((1, tile, D), lambda b, t: (b, t, 0)),
            pl.BlockSpec((GLA_HEADS, HV, HK), lambda b, t: (0, 0, 0)),
            pl.BlockSpec((D, 2 * DK + 2 * DV), const2),
            pl.BlockSpec((D, LANES), const2),
            pl.BlockSpec((LANES, DK), const2),
            pl.BlockSpec((1, DK), const2),
            pl.BlockSpec((1, HV), const2),
            pl.BlockSpec((DV, D), const2),
            pl.BlockSpec((1, D), const2),
            pl.BlockSpec((1, D), const2),
        ],
        out_specs=[
            pl.BlockSpec((1, tile, D), lambda b, t: (b, t, 0)),
            pl.BlockSpec((1, GLA_HEADS, HV, HK), lambda b, t: (b, 0, 0, 0)),
        ],
        out_shape=[
            jax.ShapeDtypeStruct((B, L, D), F32),
            jax.ShapeDtypeStruct((B, GLA_HEADS, HV, HK), F32),
        ],
        scratch_shapes=[
            pltpu.VMEM((GLA_HEADS, HV, HK), F32),
            pltpu.VMEM((tile, DK), F32),
            pltpu.VMEM((tile, DK), F32),
            pltpu.VMEM((tile, DV), BF16),
            pltpu.VMEM((tile, DV), F32),
            pltpu.VMEM((tile, DK), F32),
            pltpu.VMEM((tile, DV), BF16),
        ],
        compiler_params=pltpu.CompilerParams(
            dimension_semantics=("arbitrary", "arbitrary"),
            vmem_limit_bytes=VMEM_LIMIT_BYTES),
        name=name,
    )(h, s0, w_in, w_a1, w_a2, b_a, ng, w_out, lg, lb)


MAIN_TILE = 512
MAIN_ROWS = 256
MAIN_CHUNK = 128
CONV_TILE = 1024


def kernel(x, meta, conv_w_in, conv_b_in, conv_w_dw, conv_b_dw, conv_norm_g, conv_norm_b,
           conv_w_out, conv_b_out, gla_w_in, gla_w_a2, gla_b_a, gla_norm_g, gla_w_out,
           post_ln_g, post_ln_b):
    h = x
    hm = meta[None].astype(x.dtype)
    row = lambda v: v.reshape(1, -1)
    n_qkvr = 2 * GLA_DK + 2 * GLA_DV
    for i in range(DEPTH):
        j = i // 2
        lg, lb = row(post_ln_g[i]), row(post_ln_b[i])
        if i % 2 == 0:
            w_delay = jnp.flip(jnp.pad(conv_w_dw[j], ((CONV_HALO - CONV_TAPS, 0), (0, 0))), axis=0)
            w_dw = jnp.repeat(w_delay, SUBLANES, axis=0)
            args = (conv_w_in[j].astype(BF16), row(conv_b_in[j]), w_dw, row(conv_b_dw[j]),
                    row(conv_norm_g[j]), row(conv_norm_b[j]), conv_w_out[j].astype(BF16),
                    row(conv_b_out[j]), lg, lb)
            st0 = jnp.zeros((CONV_HALO, CONV_CH), F32)
            hm, st = _conv_layer(hm, st0, *args, tile=N_META, rows=N_META, name=f"conv{j}_meta")
            h, _ = _conv_layer(h, st[0], *args, tile=CONV_TILE, rows=MAIN_ROWS, name=f"conv{j}_main")
        else:
            w_in = gla_w_in[j]
            w_a1 = jnp.pad(w_in[:, n_qkvr:], ((0, 0), (0, LANES - GLA_GATE_RANK))).astype(BF16)
            w_a2 = jnp.pad(gla_w_a2[j], ((0, LANES - GLA_GATE_RANK), (0, 0))).astype(BF16)
            args = (w_in[:, :n_qkvr].astype(BF16), w_a1, w_a2, row(gla_b_a[j]), row(gla_norm_g[j]),
                    gla_w_out[j].astype(BF16), lg, lb)
            s0 = jnp.zeros((GLA_HEADS, GLA_HV, GLA_HK), F32)
            hm, s = _gla_layer(hm, s0, *args, tile=N_META, rows=N_META, chunk=N_META,
                               name=f"gla{j}_meta")
            h, _ = _gla_layer(h, s[0], *args, tile=MAIN_TILE, rows=MAIN_ROWS, chunk=MAIN_CHUNK,
                              name=f"gla{j}_main")
    return h
```

```python
import functools

import jax
import jax.numpy as jnp
from jax import lax
from jax.experimental import pallas as pl
from jax.experimental.pallas import tpu as pltpu

D_MODEL = 1024
DEPTH = 4
N_META = 16
DN_ALPHA = (2 * DEPTH) ** 0.25
LN_EPS = 1e-5
RMS_EPS = 1e-6
CONV_CH = D_MODEL
CONV_TAPS = 31
GLA_HEADS = 4
GLA_DK = D_MODEL // 2
GLA_DV = D_MODEL
GLA_HK = GLA_DK // GLA_HEADS
GLA_HV = GLA_DV // GLA_HEADS
GLA_GATE_RANK = 16
GLA_TAU = 16.0

LANES = 128
SUBLANES = 8
CONV_HALO = 32
VMEM_LIMIT_BYTES = 56 * 1024 * 1024

F32 = jnp.float32
BF16 = jnp.bfloat16


def _layer_norm_rows(x, g, b):
    mu = jnp.mean(x, axis=-1, keepdims=True)
    xc = x - mu
    var = jnp.mean(xc * xc, axis=-1, keepdims=True)
    return xc * lax.rsqrt(var + LN_EPS) * g + b


def _sigmoid(x):
    return 0.5 * jnp.tanh(0.5 * x) + 0.5


def _silu(x):
    return x * _sigmoid(x)


def _conv_layer_kernel(h_ref, st0_ref, w_in_ref, b_in_ref, w_dw_ref, b_dw_ref, ng_ref, nb_ref,
                       w_out_ref, b_out_ref, lg_ref, lb_ref, out_ref, st_ref,
                       glu_buf, cn_buf, *, tile, rows, clanes, n_inline):
    C = CONV_CH
    t = pl.program_id(1)

    @pl.when(t == 0)
    def _():
        glu_buf[0:CONV_HALO, :] = st0_ref[...]

    @pl.when(t > 0)
    def _():
        glu_buf[0:CONV_HALO, :] = glu_buf[tile:tile + CONV_HALO, :]

    def glu_matmuls(i):
        x = h_ref[0, i * rows:(i + 1) * rows, :].astype(BF16)
        a = jnp.dot(x, w_in_ref[:, 0:C], preferred_element_type=F32)
        ga = jnp.dot(x, w_in_ref[:, C:2 * C], preferred_element_type=F32)
        return a, ga

    def glu_store(i, a, ga):
        glu_buf[CONV_HALO + i * rows:CONV_HALO + (i + 1) * rows, :] = (
            (a + b_in_ref[:, 0:C]) * _sigmoid(ga + b_in_ref[:, C:2 * C]))

    sub = lax.broadcasted_iota(jnp.int32, (SUBLANES, clanes), 0)

    def conv_block(row, j8, lanes, bias, q_prev):
        xs = [glu_buf[pl.ds(row + (CONV_HALO - SUBLANES * (a + 1)), SUBLANES), lanes]
              for a in range(CONV_HALO // SUBLANES)]
        out = bias
        q_new = []
        for s in range(SUBLANES):
            p = None
            for a in range(CONV_HALO // SUBLANES):
                d = SUBLANES * a + s
                if d >= CONV_TAPS:
                    continue
                term = w_dw_ref[SUBLANES * d:SUBLANES * (d + 1), lanes] * xs[a]
                p = term if p is None else p + term
            if s == 0:
                out = out + p
            else:
                q = pltpu.roll(p, s, 0)
                q_new.append(q)
                out = out + jnp.where(sub >= s, q, q_prev[s - 1])
        cn_buf[pl.ds(j8, SUBLANES), lanes] = out
        return tuple(q_new)

    def conv_lanes(i, l0, inline):
        lanes = slice(l0, l0 + clanes)
        bias = jnp.broadcast_to(b_dw_ref[:, lanes], (SUBLANES, clanes))
        carry = (jnp.zeros((SUBLANES, clanes), F32),) * (SUBLANES - 1)
        n_steps = rows // SUBLANES + 1
        if inline:
            for j in range(n_steps):
                carry = conv_block(i * rows + j * SUBLANES, j * SUBLANES, lanes, bias, carry)
        else:
            def step(j, q_prev):
                j8 = pl.multiple_of(j * SUBLANES, SUBLANES)
                return conv_block(i * rows + j8, j8, lanes, bias, q_prev)
            lax.fori_loop(0, n_steps, step, carry, unroll=11 if rows >= 64 else 1)

    n_blocks = tile // rows
    lane_groups = list(range(0, C, clanes))
    n_loop = len(lane_groups) - n_inline
    glu_store(0, *glu_matmuls(0))
    for i in range(n_blocks):
        for l0 in lane_groups[:n_loop]:
            conv_lanes(i, l0, inline=False)
        for l0 in lane_groups[n_loop:]:
            conv_lanes(i, l0, inline=True)
        rs = slice(i * rows, (i + 1) * rows)
        h = h_ref[0, rs, :]
        z = jnp.dot(h.astype(BF16), w_in_ref[:, 2 * C:3 * C], preferred_element_type=F32)
        nxt = glu_matmuls(i + 1) if i + 1 < n_blocks else None
        cn = _silu(_layer_norm_rows(cn_buf[SUBLANES:SUBLANES + rows, :], ng_ref[...], nb_ref[...]))
        gated = (cn * _silu(z + b_in_ref[:, 2 * C:3 * C])).astype(BF16)
        y = jnp.dot(gated, w_out_ref[...], preferred_element_type=F32) + b_out_ref[...]
        if nxt is not None:
            glu_store(i + 1, *nxt)
        out_ref[0, rs, :] = _layer_norm_rows(DN_ALPHA * h + y, lg_ref[...], lb_ref[...])
    st_ref[0] = glu_buf[tile:tile + CONV_HALO, :]


def _conv_layer(h, st0, w_in, b_in, w_dw, b_dw, ng, nb, w_out, b_out, lg, lb, *, tile, rows, name):
    B, L, D = h.shape
    C = CONV_CH
    clanes = 256
    n_inline = 1 if tile > rows else 0
    const = lambda b, t: (0, 0)
    kern = functools.partial(_conv_layer_kernel, tile=tile, rows=rows, clanes=clanes,
                             n_inline=n_inline)
    return pl.pallas_call(
        kern,
        grid=(B, L // tile),
        in_specs=[
            pl.BlockSpec((1, tile, D), lambda b, t: (b, t, 0)),
            pl.BlockSpec((CONV_HALO, C), const),
            pl.BlockSpec((D, 3 * C), const, pipeline_mode=pl.Buffered(1)),
            pl.BlockSpec((1, 3 * C), const),
            pl.BlockSpec((CONV_HALO * SUBLANES, C), const, pipeline_mode=pl.Buffered(1)),
            pl.BlockSpec((1, C), const),
            pl.BlockSpec((1, C), const),
            pl.BlockSpec((1, C), const),
            pl.BlockSpec((C, D), const, pipeline_mode=pl.Buffered(1)),
            pl.BlockSpec((1, D), const),
            pl.BlockSpec((1, D), const),
            pl.BlockSpec((1, D), const),
        ],
        out_specs=[
            pl.BlockSpec((1, tile, D), lambda b, t: (b, t, 0)),
            pl.BlockSpec((1, CONV_HALO, C), lambda b, t: (b, 0, 0)),
        ],
        out_shape=[
            jax.ShapeDtypeStruct((B, L, D), F32),
            jax.ShapeDtypeStruct((B, CONV_HALO, C), F32),
        ],
        scratch_shapes=[
            pltpu.VMEM((tile + CONV_HALO, C), F32),
            pltpu.VMEM((rows + SUBLANES, C), F32),
        ],
        compiler_params=pltpu.CompilerParams(
            dimension_semantics=("arbitrary", "arbitrary"),
            vmem_limit_bytes=VMEM_LIMIT_BYTES),
        name=name,
    )(h, st0, w_in, b_in, w_dw, b_dw, ng, nb, w_out, b_out, lg, lb)


def _gla_layer_kernel(h_ref, s0_ref, w_in_ref, w_a1_ref, w_a2_ref, b_a_ref, ng_ref, w_out_ref,
                      lg_ref, lb_ref, out_ref, sfin_ref,
                      s_buf, q_buf, k_buf, v_buf, r_buf, g_buf, gated_buf, *, tile, rows, chunk):
    DK, DV, HK, HV = GLA_DK, GLA_DV, GLA_HK, GLA_HV
    t = pl.program_id(1)

    @pl.when(t == 0)
    def _():
        s_buf[...] = s0_ref[...]

    x = h_ref[0].astype(BF16)
    a1 = jnp.dot(x, w_a1_ref[...], preferred_element_type=F32).astype(BF16)
    q_buf[...] = jnp.dot(x, w_in_ref[:, 0:DK], preferred_element_type=F32) * (HK ** -0.5)
    k_buf[...] = jnp.dot(x, w_in_ref[:, DK:2 * DK], preferred_element_type=F32)
    xg = jnp.dot(a1, w_a2_ref[...], preferred_element_type=F32) + b_a_ref[...]
    v_buf[...] = jnp.dot(x, w_in_ref[:, 2 * DK:2 * DK + DV],
                         preferred_element_type=F32).astype(BF16)
    r_buf[...] = jnp.dot(x, w_in_ref[:, 2 * DK + DV:2 * DK + 2 * DV], preferred_element_type=F32)
    log_sig = jnp.minimum(xg, 0.0) - jnp.log1p(jnp.exp(-jnp.abs(xg)))
    g_buf[...] = log_sig * (1.0 / GLA_TAU)

    row_id = lax.broadcasted_iota(jnp.int32, (chunk, chunk), 0)
    col_id = lax.broadcasted_iota(jnp.int32, (chunk, chunk), 1)
    causal = row_id >= col_id
    tri = jnp.where(causal, 1.0, 0.0).astype(BF16)

    chunks = [slice(c0, c0 + chunk) for c0 in range(0, tile, chunk)]
    heads = [(slice(hd * HK, (hd + 1) * HK), slice(hd * HV, (hd + 1) * HV))
             for hd in range(GLA_HEADS)]
    nt_dims = (((1,), (1,)), ((), ()))
    tn_dims = (((0,), (0,)), ((), ()))
    mid = chunk // 2 - 1

    bcs = []
    for rs in chunks:
        g = g_buf[rs, :]
        g1 = g.astype(BF16)
        rem = g - g1.astype(F32)
        g2 = rem.astype(BF16)
        g3 = (rem - g2.astype(F32)).astype(BF16)
        bcs.append(jnp.dot(tri, g1, preferred_element_type=F32)
                   + jnp.dot(tri, g2, preferred_element_type=F32)
                   + jnp.dot(tri, g3, preferred_element_type=F32))

    q_decs, q_mids, k_mids, k_rems, decs = [], [], [], [], []
    for rs, bc in zip(chunks, bcs):
        b_last = bc[chunk - 1:chunk, :]
        b_mid = bc[mid:mid + 1, :]
        q = q_buf[rs, :]
        k = k_buf[rs, :]
        q_decs.append((q * jnp.exp(bc)).astype(BF16))
        q_mids.append((q * jnp.exp(bc - b_mid)).astype(BF16))
        k_mids.append((k * jnp.exp(b_mid - bc)).astype(BF16))
        k_rems.append((k * jnp.exp(b_last - bc)).astype(BF16))
        decs.append(jnp.exp(b_last))

    atts = [[jnp.where(causal,
                       lax.dot_general(q_mids[c][:, ks], k_mids[c][:, ks], nt_dims,
                                       preferred_element_type=F32), 0.0).astype(BF16)
             for ks, vs in heads] for c in range(len(chunks))]
    upds = [[lax.dot_general(v_buf[rs, vs], k_rems[c][:, ks], tn_dims,
                             preferred_element_type=F32)
             for ks, vs in heads] for c, rs in enumerate(chunks)]
    o_intras = [[jnp.dot(atts[c][hd], v_buf[rs, vs], preferred_element_type=F32)
                 for hd, (ks, vs) in enumerate(heads)] for c, rs in enumerate(chunks)]

    for c, rs in enumerate(chunks):
        for hd, (ks, vs) in enumerate(heads):
            s_t = s_buf[hd]
            o = o_intras[c][hd] + lax.dot_general(q_decs[c][:, ks], s_t.astype(BF16), nt_dims,
                                                  preferred_element_type=F32)
            s_buf[hd] = s_t * decs[c][:, ks] + upds[c][hd]
            o = o * lax.rsqrt(jnp.mean(o * o, axis=-1, keepdims=True) + RMS_EPS) * ng_ref[...]
            gated_buf[rs, vs] = (o * _silu(r_buf[rs, vs])).astype(BF16)

    for r0 in range(0, tile, rows):
        ro = slice(r0, r0 + rows)
        y = jnp.dot(gated_buf[ro, :], w_out_ref[...], preferred_element_type=F32)
        out_ref[0, ro, :] = _layer_norm_rows(
            DN_ALPHA * h_ref[0, ro, :] + y, lg_ref[...], lb_ref[...])

    sfin_ref[0] = s_buf[...]


def _gla_layer(h, s0, w_in, w_a1, w_a2, b_a, ng, w_out, lg, lb, *, tile, rows, chunk, name):
    B, L, D = h.shape
    DK, DV, HK, HV = GLA_DK, GLA_DV, GLA_HK, GLA_HV
    const2 = lambda b, t: (0, 0)
    kern = functools.partial(_gla_layer_kernel, tile=tile, rows=rows, chunk=chunk)
    return pl.pallas_call(
        kern,
        grid=(B, L // tile),
        in_specs=[
            pl.BlockSpec((1, tile, D), lambda b, t: (b, t, 0)),
            pl.BlockSpec((GLA_HEADS, HV, HK), lambda b, t: (0, 0, 0)),
            pl.BlockSpec((D, 2 * DK + 2 * DV), const2),
            pl.BlockSpec((D, LANES), const2),
            pl.BlockSpec((LANES, DK), const2),
            pl.BlockSpec((1, DK), const2),
            pl.BlockSpec((1, HV), const2),
            pl.BlockSpec((DV, D), const2),
            pl.BlockSpec((1, D), const2),
            pl.BlockSpec((1, D), const2),
        ],
        out_specs=[
            pl.BlockSpec((1, tile, D), lambda b, t: (b, t, 0)),
            pl.BlockSpec((1, GLA_HEADS, HV, HK), lambda b, t: (b, 0, 0, 0)),
        ],
        out_shape=[
            jax.ShapeDtypeStruct((B, L, D), F32),
            jax.ShapeDtypeStruct((B, GLA_HEADS, HV, HK), F32),
        ],
        scratch_shapes=[
            pltpu.VMEM((GLA_HEADS, HV, HK), F32),
            pltpu.VMEM((tile, DK), F32),
            pltpu.VMEM((tile, DK), F32),
            pltpu.VMEM((tile, DV), BF16),
            pltpu.VMEM((tile, DV), F32),
            pltpu.VMEM((tile, DK), F32),
            pltpu.VMEM((tile, DV), BF16),
        ],
        compiler_params=pltpu.CompilerParams(
            dimension_semantics=("arbitrary", "arbitrary"),
            vmem_limit_bytes=VMEM_LIMIT_BYTES),
        name=name,
    )(h, s0, w_in, w_a1, w_a2, b_a, ng, w_out, lg, lb)


MAIN_TILE = 512
MAIN_ROWS = 256
MAIN_CHUNK = 128
CONV_TILE = 2048


def kernel(x, meta, conv_w_in, conv_b_in, conv_w_dw, conv_b_dw, conv_norm_g, conv_norm_b,
           conv_w_out, conv_b_out, gla_w_in, gla_w_a2, gla_b_a, gla_norm_g, gla_w_out,
           post_ln_g, post_ln_b):
    h = x
    hm = meta[None].astype(x.dtype)
    row = lambda v: v.reshape(1, -1)
    n_qkvr = 2 * GLA_DK + 2 * GLA_DV
    for i in range(DEPTH):
        j = i // 2
        lg, lb = row(post_ln_g[i]), row(post_ln_b[i])
        if i % 2 == 0:
            w_delay = jnp.flip(jnp.pad(conv_w_dw[j], ((CONV_HALO - CONV_TAPS, 0), (0, 0))), axis=0)
            w_dw = jnp.repeat(w_delay, SUBLANES, axis=0)
            args = (conv_w_in[j].astype(BF16), row(conv_b_in[j]), w_dw, row(conv_b_dw[j]),
                    row(conv_norm_g[j]), row(conv_norm_b[j]), conv_w_out[j].astype(BF16),
                    row(conv_b_out[j]), lg, lb)
            st0 = jnp.zeros((CONV_HALO, CONV_CH), F32)
            hm, st = _conv_layer(hm, st0, *args, tile=N_META, rows=N_META, name=f"conv{j}_meta")
            h, _ = _conv_layer(h, st[0], *args, tile=CONV_TILE, rows=MAIN_ROWS, name=f"conv{j}_main")
        else:
            w_in = gla_w_in[j]
            w_a1 = jnp.pad(w_in[:, n_qkvr:], ((0, 0), (0, LANES - GLA_GATE_RANK))).astype(BF16)
            w_a2 = jnp.pad(gla_w_a2[j], ((0, LANES - GLA_GATE_RANK), (0, 0))).astype(BF16)
            args = (w_in[:, :n_qkvr].astype(BF16), w_a1, w_a2, row(gla_b_a[j]), row(gla_norm_g[j]),
                    gla_w_out[j].astype(BF16), lg, lb)
            s0 = jnp.zeros((GLA_HEADS, GLA_HV, GLA_HK), F32)
            hm, s = _gla_layer(hm, s0, *args, tile=N_META, rows=N_META, chunk=N_META,
                               name=f"gla{j}_meta")
            h, _ = _gla_layer(h, s[0], *args, tile=MAIN_TILE, rows=MAIN_ROWS, chunk=MAIN_CHUNK,
                              name=f"gla{j}_main")
    return h
```
